```python
import jax, jax.numpy as jnp
from jax import lax
import numpy as np

D_MODEL = 1024
BATCH = 4
SEQ = 8192
DEPTH = 4

HEAD_DIM = 64
N_HEADS_A = 8
N_HEADS_B = 8
IDX_HEADS = 4
IDX_DIM = 64
TOPK_MAX = 256
BLOCK_Q = 128
CHUNK = 128
C_WIDTH = 2048
C_GROUPS = 16
D_FF = 2816
EPS = 1e-6
N_EVEN = (DEPTH + 1) // 2
N_ODD = DEPTH // 2
EVEN_WIDTHS = (N_HEADS_A * HEAD_DIM, HEAD_DIM, HEAD_DIM, IDX_HEADS * IDX_DIM, IDX_DIM, IDX_HEADS,
               N_HEADS_B * HEAD_DIM, N_HEADS_B * HEAD_DIM, N_HEADS_B * HEAD_DIM)
EVEN_IN = sum(EVEN_WIDTHS)
EVEN_MIX = (N_HEADS_A + N_HEADS_B) * HEAD_DIM

kernel_name = "hybrid_dsa_stickbreak_gmlp_trunk"


def rms_norm(x, g):
    xf = x.astype(jnp.float32)
    y = xf * lax.rsqrt(jnp.mean(xf * xf, axis=-1, keepdims=True) + EPS)
    return (y * g.astype(jnp.float32)).astype(x.dtype)


def layer_norm(x, g, b):
    xf = x.astype(jnp.float32)
    mu = jnp.mean(xf, axis=-1, keepdims=True)
    var = jnp.mean(jnp.square(xf - mu), axis=-1, keepdims=True)
    y = (xf - mu) * lax.rsqrt(var + EPS)
    return (y * g.astype(jnp.float32) + b.astype(jnp.float32)).astype(x.dtype)


def to_blocks(a, blk):
    b, s = a.shape[:2]
    return jnp.moveaxis(a.reshape(b, s // blk, blk, *a.shape[2:]), 1, 0)


def from_blocks(a):
    a = jnp.moveaxis(a, 0, 1)
    return a.reshape(a.shape[0], a.shape[1] * a.shape[2], *a.shape[3:])


def dsa_attention(q, k, v, qi, ki, wi):
    s_len = k.shape[1]
    topk = min(TOPK_MAX, s_len // 4)
    key_pos = jnp.arange(s_len, dtype=jnp.int32)
    gather = jax.vmap(lambda table, idx: table[idx])

    def block(args):
        qb, qib, wib, start = args
        qpos = start + jnp.arange(BLOCK_Q, dtype=jnp.int32)
        causal = key_pos[None, :] <= qpos[:, None]
        dots = jnp.einsum('bqhd,bsd->bqhs', qib, ki).astype(jnp.float32) * (IDX_DIM ** -0.5)
        score = jnp.einsum('bqh,bqhs->bqs', wib.astype(jnp.float32), jax.nn.relu(dots))
        score = jnp.where(causal[None], score, -jnp.inf)
        _, idx = lax.top_k(score, topk)
        valid = idx <= qpos[None, :, None]
        k_sel = gather(k, idx)
        v_sel = gather(v, idx)
        logits = jnp.einsum('bqhd,bqkd->bhqk', qb, k_sel).astype(jnp.float32) * (HEAD_DIM ** -0.5)
        logits = jnp.where(valid[:, None], logits, -jnp.inf)
        p = jax.nn.softmax(logits, axis=-1).astype(v.dtype)
        return jnp.einsum('bhqk,bqkd->bqhd', p, v_sel)

    starts = jnp.arange(s_len // BLOCK_Q, dtype=jnp.int32) * BLOCK_Q
    out = lax.map(block, (to_blocks(q, BLOCK_Q), to_blocks(qi, BLOCK_Q), to_blocks(wi, BLOCK_Q), starts))
    return from_blocks(out)


def stick_breaking_attention(q, k, v):
    s_len = k.shape[1]
    key_pos = jnp.arange(s_len, dtype=jnp.int32)

    def block(args):
        qb, start = args
        qpos = start + jnp.arange(BLOCK_Q, dtype=jnp.int32)
        strict = (key_pos[None, :] < qpos[:, None])[None, None]
        z = jnp.einsum('bqhd,bshd->bhqs', qb, k).astype(jnp.float32) * (HEAD_DIM ** -0.5)
        log_fail = jnp.where(strict, jax.nn.log_sigmoid(-z), 0.0)
        after = lax.cumsum(log_fail, axis=3, reverse=True) - log_fail
        a = jnp.where(strict, jnp.exp(jax.nn.log_sigmoid(z) + after), 0.0)
        return jnp.einsum('bhqs,bshd->bqhd', a.astype(v.dtype), v)

    starts = jnp.arange(s_len // BLOCK_Q, dtype=jnp.int32) * BLOCK_Q
    out = lax.map(block, (to_blocks(q, BLOCK_Q), starts))
    return from_blocks(out)


def spatial_gating(h, w_in, w_s, b_s, g_v, b_v, w_out):
    b, s_len, _ = h.shape
    z = jax.nn.gelu(h @ w_in)
    u, v = jnp.split(z, 2, axis=-1)
    v = layer_norm(v, g_v, b_v)
    v = v.reshape(b, s_len // CHUNK, CHUNK, C_GROUPS, C_WIDTH // C_GROUPS)
    w = w_s * jnp.tril(jnp.ones((CHUNK, CHUNK), w_s.dtype))
    sv = jnp.einsum('gts,bnsgc->bntgc', w, v) + b_s.T[None, None, :, :, None]
    return (u * sv.reshape(b, s_len, C_WIDTH)) @ w_out


def swiglu(h, w_in, w_out):
    gate, up = jnp.split(h @ w_in, 2, axis=-1)
    return (jax.nn.silu(gate) * up) @ w_out


def setup_inputs(seed: int = 0) -> dict:
    key = jax.random.key(seed)
    ks = jax.random.split(key, 14)
    f32 = jnp.float32
    nrm = lambda k, shape, fan_in: jax.random.normal(k, shape, f32) * (fan_in ** -0.5)
    return {
        "x": jax.random.normal(ks[0], (BATCH, SEQ, D_MODEL), f32),
        "norm_gains": 1.0 + 0.02 * jax.random.normal(ks[1], (DEPTH, 4, D_MODEL), f32),
        "w_in_even": nrm(ks[2], (N_EVEN, D_MODEL, EVEN_IN), D_MODEL),
        "w_out_even": nrm(ks[3], (N_EVEN, EVEN_MIX, D_MODEL), EVEN_MIX),
        "w_in_odd": nrm(ks[4], (N_ODD, D_MODEL, 2 * C_WIDTH), D_MODEL),
        "spatial_w": nrm(ks[5], (N_ODD, C_GROUPS, CHUNK, CHUNK), CHUNK),
        "spatial_b": 1.0 + 0.02 * jax.random.normal(ks[6], (N_ODD, C_GROUPS, CHUNK), f32),
        "v_norm_gain": 1.0 + 0.02 * jax.random.normal(ks[7], (N_ODD, C_WIDTH), f32),
        "v_norm_bias": 0.02 * jax.random.normal(ks[8], (N_ODD, C_WIDTH), f32),
        "w_out_odd": nrm(ks[9], (N_ODD, C_WIDTH, D_MODEL), C_WIDTH),
        "w_ffn_in": nrm(ks[10], (DEPTH, D_MODEL, 2 * D_FF), D_MODEL),
        "w_ffn_out": nrm(ks[11], (DEPTH, D_FF, D_MODEL), D_FF),
    }


def reference(x, norm_gains, w_in_even, w_out_even, w_in_odd, spatial_w, spatial_b,
              v_norm_gain, v_norm_bias, w_out_odd, w_ffn_in, w_ffn_out):
    b, s_len, _ = x.shape
    split_points = np.cumsum(EVEN_WIDTHS)[:-1].tolist()
    h = x
    for layer in range(DEPTH):
        g = norm_gains[layer]
        a = rms_norm(h, g[0])
        i = layer // 2
        if layer % 2 == 0:
            proj = a @ w_in_even[i]
            qa, ka, va, qi, ki, wi, qb, kb, vb = jnp.split(proj, split_points, axis=-1)
            ya = dsa_attention(qa.reshape(b, s_len, N_HEADS_A, HEAD_DIM), ka, va,
                               qi.reshape(b, s_len, IDX_HEADS, IDX_DIM), ki,
                               wi * (IDX_HEADS ** -0.5))
            yb = stick_breaking_attention(qb.reshape(b, s_len, N_HEADS_B, HEAD_DIM),
                                          kb.reshape(b, s_len, N_HEADS_B, HEAD_DIM),
                                          vb.reshape(b, s_len, N_HEADS_B, HEAD_DIM))
            y = jnp.concatenate([ya.reshape(b, s_len, -1), yb.reshape(b, s_len, -1)], axis=-1) @ w_out_even[i]
        else:
            y = spatial_gating(a, w_in_odd[i], spatial_w[i], spatial_b[i],
                               v_norm_gain[i], v_norm_bias[i], w_out_odd[i])
        h = h + rms_norm(y, g[1])
        f = swiglu(rms_norm(h, g[2]), w_ffn_in[layer], w_ffn_out[layer])
        h = h + rms_norm(f, g[3])
    return h
```

```python
import functools

import numpy as np
import jax
import jax.numpy as jnp
from jax import lax
from jax.experimental import pallas as pl
from jax.experimental.pallas import tpu as pltpu

F32 = jnp.float32
BF16 = jnp.bfloat16
I32 = jnp.int32

D_MODEL = 1024
HEAD_DIM = 64
N_HEADS_A = 8
N_HEADS_B = 8
IDX_HEADS = 4
IDX_DIM = 64
TOPK_MAX = 256
BLOCK_Q = 128
CHUNK = 128
C_WIDTH = 2048
C_GROUPS = 16
D_FF = 2816
EPS = 1e-6
EVEN_WIDTHS = (N_HEADS_A * HEAD_DIM, HEAD_DIM, HEAD_DIM, IDX_HEADS * IDX_DIM, IDX_DIM, IDX_HEADS,
               N_HEADS_B * HEAD_DIM, N_HEADS_B * HEAD_DIM, N_HEADS_B * HEAD_DIM)

LANES = 128
VMEM_LIMIT = 48 * 1024 * 1024

COL_QA = 0
COL_QI = 512
COL_KVA = 768
COL_KI = 896
COL_QB = 1024
COL_KB = 1536
COL_VB = 2048
EVEN_COLS = 2560

MASKED = -1e30
INT_MIN = -2 ** 31
NEG_INF_KEY = -2139095041
NO_INDEX = 2 ** 30


def _params(*sem):
    return pltpu.CompilerParams(dimension_semantics=sem, vmem_limit_bytes=VMEM_LIMIT)


def _dot(a, b):
    return jnp.dot(a, b, preferred_element_type=F32)


def _dot_nt(a, b):
    return lax.dot_general(a, b, (((1,), (1,)), ((), ())), preferred_element_type=F32)


def _rms(y, g):
    ms = jnp.mean(y * y, axis=-1, keepdims=True)
    return (y * lax.rsqrt(ms + EPS)) * g


def _gelu_tanh(x):
    c = np.float32(np.sqrt(2.0 / np.pi))
    return x * (0.5 * (1.0 + jnp.tanh(c * (x + 0.044715 * (x * x * x)))))


def _norm_matmul_kernel(x_ref, g_ref, *refs, act):
    n_w = 2 if act == "swiglu" else 1
    w_refs, o_ref, xn_ref = refs[:n_w], refs[n_w], refs[n_w + 1]

    @pl.when(pl.program_id(1) == 0)
    def _():
        xn_ref[...] = _rms(x_ref[...], g_ref[...]).astype(BF16)

    xn = xn_ref[...]
    y = _dot(xn, w_refs[0][...])
    if act == "gelu":
        y = _gelu_tanh(y)
    elif act == "swiglu":
        up = _dot(xn, w_refs[1][...])
        y = (y / (1.0 + jnp.exp(-y))) * up
    o_ref[...] = y.astype(o_ref.dtype)


def _norm_matmul(x, g, w, *, act, tm, tn, out_dtype, n_out=None):
    t, d = x.shape
    n_out = w.shape[1] if n_out is None else n_out
    in_specs = [pl.BlockSpec((tm, d), lambda i, j: (i, 0)),
                pl.BlockSpec((1, d), lambda i, j: (0, 0)),
                pl.BlockSpec((d, tn), lambda i, j: (0, j))]
    args = [x, g, w]
    if act == "swiglu":
        up_off = n_out // tn
        in_specs.append(pl.BlockSpec((d, tn), lambda i, j: (0, j + up_off)))
        args.append(w)
    return pl.pallas_call(
        functools.partial(_norm_matmul_kernel, act=act),
        grid=(t // tm, n_out // tn),
        in_specs=in_specs,
        out_specs=pl.BlockSpec((tm, tn), lambda i, j: (i, j)),
        out_shape=jax.ShapeDtypeStruct((t, n_out), out_dtype),
        scratch_shapes=[pltpu.VMEM((tm, d), BF16)],
        compiler_params=_params("parallel", "arbitrary"),
        name="norm_matmul_" + act,
    )(*args)


def _mm_norm_res_kernel(*refs, n_in):
    x_refs, w_refs = refs[:n_in], refs[n_in:2 * n_in]
    g_ref, h_ref, o_ref = refs[2 * n_in:]
    y = _dot(x_refs[0][...], w_refs[0][...])
    for k in range(1, n_in):
        y = y + _dot(x_refs[k][...], w_refs[k][...])
    o_ref[...] = h_ref[...] + _rms(y, g_ref[...])


def _mm_norm_res(xs, ws, g, h, *, tm):
    t, d = h.shape
    n_in = len(xs)
    in_specs = [pl.BlockSpec((tm, x.shape[1]), lambda i: (i, 0)) for x in xs]
    in_specs += [pl.BlockSpec(w.shape, lambda i: (0, 0)) for w in ws]
    in_specs += [pl.BlockSpec((1, d), lambda i: (0, 0)), pl.BlockSpec((tm, d), lambda i: (i, 0))]
    return pl.pallas_call(
        functools.partial(_mm_norm_res_kernel, n_in=n_in),
        grid=(t // tm,),
        in_specs=in_specs,
        out_specs=pl.BlockSpec((tm, d), lambda i: (i, 0)),
        out_shape=jax.ShapeDtypeStruct((t, d), F32),
        compiler_params=_params("parallel"),
        name="matmul_norm_residual",
    )(*xs, *ws, g, h)


def _tile_lanes(x, n):
    return x if n == 1 else jnp.concatenate([x] * n, axis=1)


def _dsa_kernel(q_ref, qi_ref, kv_ref, ki_ref, wi_ref, o_ref,
                key_ref, tie_ref, qh_ref, qih_ref, m_ref, l_ref, acc_ref,
                *, tq, tk, topk, idx_bits):
    i = pl.program_id(1)
    q0 = i * tq
    n_t = q0 // tk + 1
    n_slab = tk // LANES
    lane = lax.broadcasted_iota(I32, (tq, LANES), 1)
    low_half = lane < HEAD_DIM

    for pair in range(N_HEADS_A // 2):
        qp = q_ref[:, pair * LANES:(pair + 1) * LANES].astype(F32)
        qh_ref[2 * pair] = jnp.where(low_half, qp, 0.0).astype(BF16)
        qh_ref[2 * pair + 1] = jnp.where(low_half, pltpu.roll(qp, HEAD_DIM, 1), 0.0).astype(BF16)
    for pair in range(IDX_HEADS // 2):
        qp = qi_ref[:, pair * LANES:(pair + 1) * LANES].astype(F32)
        qih_ref[2 * pair] = qp.astype(BF16)
        qih_ref[2 * pair + 1] = pltpu.roll(qp, HEAD_DIM, 1).astype(BF16)

    row = lax.broadcasted_iota(I32, (tq, tk), 0) + q0
    col0 = lax.broadcasted_iota(I32, (tq, tk), 1)

    def score_body(t, carry):
        off = pl.multiple_of(t * tk, tk)
        ki = ki_ref[pl.ds(off, tk), :]
        sc = jnp.zeros((tq, tk), F32)
        for h in range(IDX_HEADS):
            d = _dot_nt(qih_ref[h], ki)
            wb = wi_ref[:, h * LANES:(h + 1) * LANES]
            sc = sc + jnp.maximum(d, 0.0) * _tile_lanes(wb, n_slab)
        sc = jnp.where(col0 + off <= row, sc, -jnp.inf)
        b = pltpu.bitcast(sc, I32)
        key = b ^ ((b >> 31) & 0x7FFFFFFF)
        key_ref[:, pl.ds(off, tk)] = jnp.where(key == -1, 0, key)
        return carry

    lax.fori_loop(0, n_t, score_body, 0)

    def count(ref, pred):
        def body(t, acc):
            off = t * tk
            for j in range(n_slab):
                x = ref[:, pl.ds(pl.multiple_of(off + j * LANES, LANES), LANES)]
                acc = acc + jnp.where(pred(x), 1.0, 0.0)
            return acc
        acc = lax.fori_loop(0, n_t, body, jnp.zeros((tq, LANES), F32))
        return jnp.broadcast_to(jnp.sum(acc, axis=1, keepdims=True), (tq, LANES))

    kf = np.float32(topk)
    thr = jnp.where(count(key_ref, lambda x: x >= 0) >= kf, 0, INT_MIN).astype(I32)

    def thr_body(p, thr):
        cand = thr | jnp.left_shift(jnp.int32(1), 30 - p)
        return jnp.where(count(key_ref, lambda x: x >= cand) >= kf, cand, thr)

    thr = lax.fori_loop(0, 31, thr_body, thr)

    need = kf - count(key_ref, lambda x: x > thr)

    def tie_body(t, carry):
        off = t * tk
        for j in range(n_slab):
            o = pl.multiple_of(off + j * LANES, LANES)
            x = key_ref[:, pl.ds(o, LANES)]
            pos = jnp.where(x != NEG_INF_KEY, lane + o, NO_INDEX)
            tie_ref[:, pl.ds(o, LANES)] = jnp.where(x == thr, pos, NO_INDEX)
        return carry

    lax.fori_loop(0, n_t, tie_body, 0)

    def cut_body(p, cut):
        cand = cut | jnp.left_shift(jnp.int32(1), idx_bits - 1 - p)
        return jnp.where(count(tie_ref, lambda x: x < cand) < need, cand, cut)

    cut = lax.fori_loop(0, idx_bits, cut_body, jnp.zeros((tq, LANES), I32))

    m_ref[...] = jnp.full(m_ref.shape, MASKED, F32)
    l_ref[...] = jnp.zeros(l_ref.shape, F32)
    acc_ref[...] = jnp.zeros(acc_ref.shape, F32)

    def att_body(t, carry):
        off = pl.multiple_of(t * tk, tk)
        kv = kv_ref[pl.ds(off, tk), :]
        slabs = []
        for j in range(n_slab):
            o = pl.multiple_of(off + j * LANES, LANES)
            x = key_ref[:, pl.ds(o, LANES)]
            e = tie_ref[:, pl.ds(o, LANES)]
            slabs.append(jnp.where(x > thr, 0.0, jnp.where(e <= cut, 0.0, MASKED)))
        bias = jnp.concatenate(slabs, axis=1)
        for h in range(N_HEADS_A):
            s = _dot_nt(qh_ref[h], kv) + bias
            m_old = m_ref[h]
            m_new = jnp.maximum(m_old, jnp.max(s, axis=1, keepdims=True))
            p = jnp.exp(s - m_new)
            alpha = jnp.exp(m_old - m_new)
            l_ref[h] = alpha * l_ref[h] + jnp.sum(p, axis=1, keepdims=True)
            acc_ref[h] = alpha * acc_ref[h] + _dot(p.astype(BF16), kv)
            m_ref[h] = m_new
        return carry

    lax.fori_loop(0, n_t, att_body, 0)

    for pair in range(N_HEADS_A // 2):
        even = acc_ref[2 * pair] / l_ref[2 * pair]
        odd = acc_ref[2 * pair + 1] / l_ref[2 * pair + 1]
        out = jnp.where(low_half, pltpu.roll(even, HEAD_DIM, 1), odd)
        o_ref[:, pair * LANES:(pair + 1) * LANES] = out.astype(o_ref.dtype)


def _dsa_attention(proj, wi, *, tq=BLOCK_Q, tk=512):
    b, s, _ = proj.shape
    topk = min(TOPK_MAX, s // 4)
    idx_bits = int(np.log2(s))
    assert 2 ** idx_bits == s and s % tk == 0 and tk % tq == 0
    qa_w = N_HEADS_A * HEAD_DIM
    qi_w = IDX_HEADS * IDX_DIM
    return pl.pallas_call(
        functools.partial(_dsa_kernel, tq=tq, tk=tk, topk=topk, idx_bits=idx_bits),
        grid=(b, s // tq),
        in_specs=[
            pl.BlockSpec((None, tq, qa_w), lambda bi, i: (bi, i, COL_QA // qa_w)),
            pl.BlockSpec((None, tq, qi_w), lambda bi, i: (bi, i, COL_QI // qi_w)),
            pl.BlockSpec((None, s, LANES), lambda bi, i: (bi, 0, COL_KVA // LANES)),
            pl.BlockSpec((None, s, LANES), lambda bi, i: (bi, 0, COL_KI // LANES)),
            pl.BlockSpec((None, tq, IDX_HEADS * LANES), lambda bi, i: (bi, i, 0)),
        ],
        out_specs=pl.BlockSpec((None, tq, qa_w), lambda bi, i: (bi, i, 0)),
        out_shape=jax.ShapeDtypeStruct((b, s, qa_w), BF16),
        scratch_shapes=[
            pltpu.VMEM((tq, s), I32),
            pltpu.VMEM((tq, s), I32),
            pltpu.VMEM((N_HEADS_A, tq, LANES), BF16),
            pltpu.VMEM((IDX_HEADS, tq, LANES), BF16),
            pltpu.VMEM((N_HEADS_A, tq, 1), F32),
            pltpu.VMEM((N_HEADS_A, tq, 1), F32),
            pltpu.VMEM((N_HEADS_A, tq, LANES), F32),
        ],
        compiler_params=_params("parallel", "arbitrary"),
        name="dsa_attention",
    )(proj, proj, proj, proj, wi)


def _sb_kernel(q_ref, k_ref, v_ref, u_ref, o_ref, *, tq, tk):
    i = pl.program_id(2)
    q0 = i * tq
    n_t = q0 // tk + 1
    lane = lax.broadcasted_iota(I32, (tq, LANES), 1)
    low_half = lane < HEAD_DIM
    q = q_ref[...]
    zero = jnp.zeros_like(q)
    qh = (jnp.where(low_half, q, zero), jnp.where(low_half, zero, q))
    row = lax.broadcasted_iota(I32, (tq, tk), 0) + q0
    col0 = lax.broadcasted_iota(I32, (tq, tk), 1)
    upper = u_ref[...]

    def tile(t, carry, masked):
        off = pl.multiple_of(t * tk, tk)
        k_t = k_ref[pl.ds(off, tk), :]
        v_t = v_ref[pl.ds(off, tk), :]
        strict = (col0 + off) < row
        out = []
        for h in range(2):
            c, acc = carry[2 * h], carry[2 * h + 1]
            z = _dot_nt(qh[h], k_t)
            log_fail = -(jnp.maximum(z, 0.0) + jnp.log1p(jnp.exp(-jnp.abs(z))))
            log_hit = log_fail + z
            if masked:
                log_fail = jnp.where(strict, log_fail, 0.0)
            hi = log_fail.astype(BF16)
            lo = (log_fail - hi.astype(F32)).astype(BF16)
            after = (_dot(hi, upper) + _dot(lo, upper)) + c
            a = jnp.exp(log_hit + after)
            if masked:
                a = jnp.where(strict, a, 0.0)
            acc = acc + _dot(a.astype(BF16), v_t)
            c = c + jnp.sum(log_fail, axis=1, keepdims=True)
            out += [c, acc]
        return tuple(out)

    zc = jnp.zeros((tq, 1), F32)
    za = jnp.zeros((tq, LANES), F32)
    carry = tile(n_t - 1, (zc, za, zc, za), True)
    carry = lax.fori_loop(0, n_t - 1, lambda j, c: tile(n_t - 2 - j, c, False), carry)
    o_ref[...] = jnp.where(low_half, carry[1], carry[3]).astype(o_ref.dtype)


def _sb_attention(proj, *, tq=BLOCK_Q, tk=256):
    b, s, _ = proj.shape
    assert s % tk == 0 and tk % tq == 0
    n_pairs = N_HEADS_B // 2
    upper = (np.arange(tk)[:, None] > np.arange(tk)[None, :]).astype(np.float32)
    return pl.pallas_call(
        functools.partial(_sb_kernel, tq=tq, tk=tk),
        grid=(b, n_pairs, s // tq),
        in_specs=[
            pl.BlockSpec((None, tq, LANES), lambda bi, p, i: (bi, i, COL_QB // LANES + p)),
            pl.BlockSpec((None, s, LANES), lambda bi, p, i: (bi, 0, COL_KB // LANES + p)),
            pl.BlockSpec((None, s, LANES), lambda bi, p, i: (bi, 0, COL_VB // LANES + p)),
            pl.BlockSpec((tk, tk), lambda bi, p, i: (0, 0)),
        ],
        out_specs=pl.BlockSpec((None, tq, LANES), lambda bi, p, i: (bi, i, p)),
        out_shape=jax.ShapeDtypeStruct((b, s, N_HEADS_B * HEAD_DIM), BF16),
        compiler_params=_params("parallel", "parallel", "arbitrary"),
        name="stick_breaking_attention",
    )(proj, proj, proj, jnp.asarray(upper, BF16))


def _gmlp_tail_kernel(u_ref, v_ref, gv_ref, bv_ref, ws_ref, bs_ref, wo_ref, g_ref, h_ref, o_ref,
                      wt_ref, vn_ref, gated_ref, *, tm):
    @pl.when(pl.program_id(0) == 0)
    def _():
        r = lax.broadcasted_iota(I32, (CHUNK, CHUNK), 0)
        c = lax.broadcasted_iota(I32, (CHUNK, CHUNK), 1)
        for g in range(C_GROUPS):
            wt_ref[g] = jnp.where(c <= r, ws_ref[g], 0.0).astype(BF16)

    v = v_ref[...].astype(F32)
    mu = jnp.mean(v, axis=-1, keepdims=True)
    var = jnp.mean(jnp.square(v - mu), axis=-1, keepdims=True)
    vn_ref[...] = (((v - mu) * lax.rsqrt(var + EPS)) * gv_ref[...] + bv_ref[...]).astype(BF16)
    cw = C_WIDTH // C_GROUPS
    for ch in range(tm // CHUNK):
        rows = slice(ch * CHUNK, (ch + 1) * CHUNK)
        for g in range(C_GROUPS):
            cols = slice(g * cw, (g + 1) * cw)
            sv = _dot(wt_ref[g], vn_ref[rows, cols]) + bs_ref[g]
            gated_ref[rows, cols] = (u_ref[rows, cols].astype(F32) * sv).astype(BF16)
    y = _dot(gated_ref[...], wo_ref[...])
    o_ref[...] = h_ref[...] + _rms(y, g_ref[...])


def _gmlp_tail(z, gv, bv, ws, bs_b, wo, g, h, *, tm=256):
    t, d = h.shape
    cw = C_WIDTH // C_GROUPS
    return pl.pallas_call(
        functools.partial(_gmlp_tail_kernel, tm=tm),
        grid=(t // tm,),
        in_specs=[
            pl.BlockSpec((tm, C_WIDTH), lambda i: (i, 0)),
            pl.BlockSpec((tm, C_WIDTH), lambda i: (i, 1)),
            pl.BlockSpec((1, C_WIDTH), lambda i: (0, 0)),
            pl.BlockSpec((1, C_WIDTH), lambda i: (0, 0)),
            pl.BlockSpec((C_GROUPS, CHUNK, CHUNK), lambda i: (0, 0, 0)),
            pl.BlockSpec((C_GROUPS, CHUNK, cw), lambda i: (0, 0, 0)),
            pl.BlockSpec((C_WIDTH, d), lambda i: (0, 0)),
            pl.BlockSpec((1, d), lambda i: (0, 0)),
            pl.BlockSpec((tm, d), lambda i: (i, 0)),
        ],
        out_specs=pl.BlockSpec((tm, d), lambda i: (i, 0)),
        out_shape=jax.ShapeDtypeStruct((t, d), F32),
        scratch_shapes=[
            pltpu.VMEM((C_GROUPS, CHUNK, CHUNK), BF16),
            pltpu.VMEM((tm, C_WIDTH), BF16),
            pltpu.VMEM((tm, C_WIDTH), BF16),
        ],
        compiler_params=_params("arbitrary"),
        name="gmlp_tail",
    )(z, z, gv, bv, ws, bs_b, wo, g, h)


def _prep_even_weights(w):
    split_points = np.cumsum(EVEN_WIDTHS)[:-1].tolist()
    qa, ka, va, qi, ki, wi, qb, kb, vb = jnp.split(w, split_points, axis=1)
    d = w.shape[0]
    main = jnp.concatenate(
        [qa * (HEAD_DIM ** -0.5), qi * (IDX_DIM ** -0.5), ka, va, ki, jnp.zeros((d, LANES - IDX_DIM), w.dtype),
         qb * (HEAD_DIM ** -0.5), kb, vb], axis=1).astype(BF16)
    wi_cols = jnp.repeat(wi * (IDX_HEADS ** -0.5), LANES, axis=1).astype(BF16)
    return main, wi_cols


def kernel(x, norm_gains, w_in_even, w_out_even, w_in_odd, spatial_w, spatial_b, v_norm_gain, v_norm_bias,
           w_out_odd, w_ffn_in, w_ffn_out):
    b, s, d = x.shape
    t = b * s
    depth = norm_gains.shape[0]
    h = x.reshape(t, d)
    tm = 512
    mix_a = N_HEADS_A * HEAD_DIM
    for layer in range(depth):
        g = norm_gains[layer][:, None, :]
        i = layer // 2
        if layer % 2 == 0:
            w_main, w_wi = _prep_even_weights(w_in_even[i])
            proj = _norm_matmul(h, g[0], w_main, act="none", tm=tm, tn=EVEN_COLS // 2, out_dtype=BF16)
            wi = _norm_matmul(h, g[0], w_wi, act="none", tm=tm, tn=IDX_HEADS * LANES, out_dtype=F32)
            proj = proj.reshape(b, s, EVEN_COLS)
            ya = _dsa_attention(proj, wi.reshape(b, s, IDX_HEADS * LANES))
            yb = _sb_attention(proj)
            wo = w_out_even[i].astype(BF16)
            h = _mm_norm_res([ya.reshape(t, mix_a), yb.reshape(t, -1)], [wo[:mix_a], wo[mix_a:]], g[1], h, tm=tm)
        else:
            z = _norm_matmul(h, g[0], w_in_odd[i].astype(BF16), act="gelu", tm=tm, tn=1024, out_dtype=BF16)
            bs_b = jnp.broadcast_to(spatial_b[i][:, :, None], (C_GROUPS, CHUNK, C_WIDTH // C_GROUPS))
            h = _gmlp_tail(z, v_norm_gain[i][None, :], v_norm_bias[i][None, :], spatial_w[i], bs_b,
                           w_out_odd[i].astype(BF16), g[1], h)
        act = _norm_matmul(h, g[2], w_ffn_in[layer].astype(BF16), act="swiglu", tm=tm, tn=D_FF // 2,
                           out_dtype=BF16, n_out=D_FF)
        h = _mm_norm_res([act], [w_ffn_out[layer].astype(BF16)], g[3], h, tm=tm)
    return h.reshape(b, s, d)
```

```python
import functools

import numpy as np
import jax
import jax.numpy as jnp
from jax import lax
from jax.experimental import pallas as pl
from jax.experimental.pallas import tpu as pltpu

F32 = jnp.float32
BF16 = jnp.bfloat16
I32 = jnp.int32

D_MODEL = 1024
HEAD_DIM = 64
N_HEADS_A = 8
N_HEADS_B = 8
IDX_HEADS = 4
IDX_DIM = 64
TOPK_MAX = 256
BLOCK_Q = 128
CHUNK = 128
C_WIDTH = 2048
C_GROUPS = 16
D_FF = 2816
EPS = 1e-6
EVEN_WIDTHS = (N_HEADS_A * HEAD_DIM, HEAD_DIM, HEAD_DIM, IDX_HEADS * IDX_DIM, IDX_DIM, IDX_HEADS,
               N_HEADS_B * HEAD_DIM, N_HEADS_B * HEAD_DIM, N_HEADS_B * HEAD_DIM)

LANES = 128
VMEM_LIMIT = 48 * 1024 * 1024

COL_QA = 0
COL_QI = 512
COL_KVA = 768
COL_KI = 896
COL_QB = 1024
COL_KB = 1536
COL_VB = 2048
EVEN_COLS = 2560

LOG2_E = float(np.log2(np.e))
MASKED = -1e30
INT_MIN = -2 ** 31
NEG_INF_KEY = -2139095041
NO_INDEX = 2 ** 30


def _params(*sem):
    return pltpu.CompilerParams(dimension_semantics=sem, vmem_limit_bytes=VMEM_LIMIT)


def _dot(a, b):
    return jnp.dot(a, b, preferred_element_type=F32)


def _dot_nt(a, b):
    return lax.dot_general(a, b, (((1,), (1,)), ((), ())), preferred_element_type=F32)


def _rms(y, g):
    ms = jnp.mean(y * y, axis=-1, keepdims=True)
    return (y * lax.rsqrt(ms + EPS)) * g


def _gelu_tanh(x):
    c = np.float32(np.sqrt(2.0 / np.pi))
    return x * (0.5 * (1.0 + jnp.tanh(c * (x + 0.044715 * (x * x * x)))))


def _norm_matmul_kernel(x_ref, g_ref, *refs, act):
    n_w = 2 if act == "swiglu" else 1
    w_refs, o_ref, xn_ref = refs[:n_w], refs[n_w], refs[n_w + 1]

    @pl.when(pl.program_id(1) == 0)
    def _():
        xn_ref[...] = _rms(x_ref[...], g_ref[...]).astype(BF16)

    xn = xn_ref[...]
    y = _dot(xn, w_refs[0][...])
    if act == "gelu":
        y = _gelu_tanh(y)
    elif act == "swiglu":
        up = _dot(xn, w_refs[1][...])
        y = (y / (1.0 + jnp.exp(-y))) * up
    o_ref[...] = y.astype(o_ref.dtype)


def _norm_matmul(x, g, w, *, act, tm, tn, out_dtype, n_out=None):
    t, d = x.shape
    n_out = w.shape[1] if n_out is None else n_out
    in_specs = [pl.BlockSpec((tm, d), lambda i, j: (i, 0)),
                pl.BlockSpec((1, d), lambda i, j: (0, 0)),
                pl.BlockSpec((d, tn), lambda i, j: (0, j))]
    args = [x, g, w]
    if act == "swiglu":
        up_off = n_out // tn
        in_specs.append(pl.BlockSpec((d, tn), lambda i, j: (0, j + up_off)))
        args.append(w)
    return pl.pallas_call(
        functools.partial(_norm_matmul_kernel, act=act),
        grid=(t // tm, n_out // tn),
        in_specs=in_specs,
        out_specs=pl.BlockSpec((tm, tn), lambda i, j: (i, j)),
        out_shape=jax.ShapeDtypeStruct((t, n_out), out_dtype),
        scratch_shapes=[pltpu.VMEM((tm, d), BF16)],
        compiler_params=_params("parallel", "arbitrary"),
        name="norm_matmul_" + act,
    )(*args)


def _mm_norm_res_kernel(*refs, n_in):
    x_refs, w_refs = refs[:n_in], refs[n_in:2 * n_in]
    g_ref, h_ref, o_ref = refs[2 * n_in:]
    y = _dot(x_refs[0][...], w_refs[0][...])
    for k in range(1, n_in):
        y = y + _dot(x_refs[k][...], w_refs[k][...])
    o_ref[...] = h_ref[...] + _rms(y, g_ref[...])


def _mm_norm_res(xs, ws, g, h, *, tm):
    t, d = h.shape
    n_in = len(xs)
    in_specs = [pl.BlockSpec((tm, x.shape[1]), lambda i: (i, 0)) for x in xs]
    in_specs += [pl.BlockSpec(w.shape, lambda i: (0, 0)) for w in ws]
    in_specs += [pl.BlockSpec((1, d), lambda i: (0, 0)), pl.BlockSpec((tm, d), lambda i: (i, 0))]
    return pl.pallas_call(
        functools.partial(_mm_norm_res_kernel, n_in=n_in),
        grid=(t // tm,),
        in_specs=in_specs,
        out_specs=pl.BlockSpec((tm, d), lambda i: (i, 0)),
        out_shape=jax.ShapeDtypeStruct((t, d), F32),
        compiler_params=_params("parallel"),
        name="matmul_norm_residual",
    )(*xs, *ws, g, h)


def _tile_lanes(x, n):
    return x if n == 1 else jnp.concatenate([x] * n, axis=1)


def _dsa_kernel(q_ref, qi_ref, kv_ref, ki_ref, wi_ref, o_ref,
                key_ref, tie_ref, qh_ref, qih_ref, m_ref, l_ref, acc_ref,
                *, tq, tk, topk, idx_bits):
    i = pl.program_id(1)
    q0 = i * tq
    n_t = q0 // tk + 1
    n_slab = tk // LANES
    lane = lax.broadcasted_iota(I32, (tq, LANES), 1)
    low_half = lane < HEAD_DIM

    for pair in range(N_HEADS_A // 2):
        qp = q_ref[:, pair * LANES:(pair + 1) * LANES].astype(F32)
        qh_ref[2 * pair] = jnp.where(low_half, qp, 0.0).astype(BF16)
        qh_ref[2 * pair + 1] = jnp.where(low_half, pltpu.roll(qp, HEAD_DIM, 1), 0.0).astype(BF16)
    for pair in range(IDX_HEADS // 2):
        qp = qi_ref[:, pair * LANES:(pair + 1) * LANES].astype(F32)
        qih_ref[2 * pair] = qp.astype(BF16)
        qih_ref[2 * pair + 1] = pltpu.roll(qp, HEAD_DIM, 1).astype(BF16)

    row = lax.broadcasted_iota(I32, (tq, tk), 0) + q0
    col0 = lax.broadcasted_iota(I32, (tq, tk), 1)

    def score_body(t, carry):
        off = pl.multiple_of(t * tk, tk)
        ki = ki_ref[pl.ds(off, tk), :]
        sc = jnp.zeros((tq, tk), F32)
        for h in range(IDX_HEADS):
            d = _dot_nt(qih_ref[h], ki)
            wb = wi_ref[:, h * LANES:(h + 1) * LANES]
            sc = sc + jnp.maximum(d, 0.0) * _tile_lanes(wb, n_slab)
        sc = jnp.where(col0 + off <= row, sc, -jnp.inf)
        b = pltpu.bitcast(sc, I32)
        key = b ^ ((b >> 31) & 0x7FFFFFFF)
        key_ref[:, pl.ds(off, tk)] = jnp.where(key == -1, 0, key)
        return carry

    lax.fori_loop(0, n_t, score_body, 0)

    def count(ref, pred):
        def body(t, acc):
            off = t * tk
            for j in range(n_slab):
                x = ref[:, pl.ds(pl.multiple_of(off + j * LANES, LANES), LANES)]
                acc = acc + jnp.where(pred(x), 1.0, 0.0)
            return acc
        acc = lax.fori_loop(0, n_t, body, jnp.zeros((tq, LANES), F32))
        return jnp.broadcast_to(jnp.sum(acc, axis=1, keepdims=True), (tq, LANES))

    kf = np.float32(topk)
    thr = jnp.where(count(key_ref, lambda x: x >= 0) >= kf, 0, INT_MIN).astype(I32)

    def thr_body(p, thr):
        cand = thr | jnp.left_shift(jnp.int32(1), 30 - p)
        return jnp.where(count(key_ref, lambda x: x >= cand) >= kf, cand, thr)

    thr = lax.fori_loop(0, 31, thr_body, thr)

    need = kf - count(key_ref, lambda x: x > thr)

    def tie_body(t, carry):
        off = t * tk
        for j in range(n_slab):
            o = pl.multiple_of(off + j * LANES, LANES)
            x = key_ref[:, pl.ds(o, LANES)]
            pos = jnp.where(x != NEG_INF_KEY, lane + o, NO_INDEX)
            tie_ref[:, pl.ds(o, LANES)] = jnp.where(x == thr, pos, NO_INDEX)
        return carry

    lax.fori_loop(0, n_t, tie_body, 0)

    def cut_body(p, cut):
        cand = cut | jnp.left_shift(jnp.int32(1), idx_bits - 1 - p)
        return jnp.where(count(tie_ref, lambda x: x < cand) < need, cand, cut)

    cut = lax.fori_loop(0, idx_bits, cut_body, jnp.zeros((tq, LANES), I32))

    m_ref[...] = jnp.full(m_ref.shape, MASKED, F32)
    l_ref[...] = jnp.zeros(l_ref.shape, F32)
    acc_ref[...] = jnp.zeros(acc_ref.shape, F32)

    def att_body(t, carry):
        off = pl.multiple_of(t * tk, tk)
        kv = kv_ref[pl.ds(off, tk), :]
        slabs = []
        for j in range(n_slab):
            o = pl.multiple_of(off + j * LANES, LANES)
            x = key_ref[:, pl.ds(o, LANES)]
            e = tie_ref[:, pl.ds(o, LANES)]
            slabs.append(jnp.where(x > thr, 0.0, jnp.where(e <= cut, 0.0, MASKED)))
        bias = jnp.concatenate(slabs, axis=1)
        for h in range(N_HEADS_A):
            s = _dot_nt(qh_ref[h], kv) + bias
            m_old = m_ref[h]
            m_new = jnp.maximum(m_old, jnp.max(s, axis=1, keepdims=True))
            p = jnp.exp(s - m_new)
            alpha = jnp.exp(m_old - m_new)
            l_ref[h] = alpha * l_ref[h] + jnp.sum(p, axis=1, keepdims=True)
            acc_ref[h] = alpha * acc_ref[h] + _dot(p.astype(BF16), kv)
            m_ref[h] = m_new
        return carry

    lax.fori_loop(0, n_t, att_body, 0)

    for pair in range(N_HEADS_A // 2):
        even = acc_ref[2 * pair] / l_ref[2 * pair]
        odd = acc_ref[2 * pair + 1] / l_ref[2 * pair + 1]
        out = jnp.where(low_half, pltpu.roll(even, HEAD_DIM, 1), odd)
        o_ref[:, pair * LANES:(pair + 1) * LANES] = out.astype(o_ref.dtype)


def _dsa_attention(proj, wi, *, tq=BLOCK_Q, tk=512):
    b, s, _ = proj.shape
    topk = min(TOPK_MAX, s // 4)
    idx_bits = int(np.log2(s))
    assert 2 ** idx_bits == s and s % tk == 0 and tk % tq == 0
    qa_w = N_HEADS_A * HEAD_DIM
    qi_w = IDX_HEADS * IDX_DIM
    return pl.pallas_call(
        functools.partial(_dsa_kernel, tq=tq, tk=tk, topk=topk, idx_bits=idx_bits),
        grid=(b, s // tq),
        in_specs=[
            pl.BlockSpec((None, tq, qa_w), lambda bi, i: (bi, i, COL_QA // qa_w)),
            pl.BlockSpec((None, tq, qi_w), lambda bi, i: (bi, i, COL_QI // qi_w)),
            pl.BlockSpec((None, s, LANES), lambda bi, i: (bi, 0, COL_KVA // LANES)),
            pl.BlockSpec((None, s, LANES), lambda bi, i: (bi, 0, COL_KI // LANES)),
            pl.BlockSpec((None, tq, IDX_HEADS * LANES), lambda bi, i: (bi, i, 0)),
        ],
        out_specs=pl.BlockSpec((None, tq, qa_w), lambda bi, i: (bi, i, 0)),
        out_shape=jax.ShapeDtypeStruct((b, s, qa_w), BF16),
        scratch_shapes=[
            pltpu.VMEM((tq, s), I32),
            pltpu.VMEM((tq, s), I32),
            pltpu.VMEM((N_HEADS_A, tq, LANES), BF16),
            pltpu.VMEM((IDX_HEADS, tq, LANES), BF16),
            pltpu.VMEM((N_HEADS_A, tq, 1), F32),
            pltpu.VMEM((N_HEADS_A, tq, 1), F32),
            pltpu.VMEM((N_HEADS_A, tq, LANES), F32),
        ],
        compiler_params=_params("parallel", "arbitrary"),
        name="dsa_attention",
    )(proj, proj, proj, proj, wi)


def _sb_kernel(q_ref, k_ref, v_ref, u_ref, o_ref, c_ref, acc_ref, *, tq, tk):
    i = pl.program_id(1)
    q0 = i * tq
    n_t = (q0 + tq - 1) // tk + 1
    n_pairs = N_HEADS_B // 2
    lane = lax.broadcasted_iota(I32, (tq, LANES), 1)
    low_half = lane < HEAD_DIM
    zero = jnp.zeros((tq, LANES), BF16)
    qh = []
    for p in range(n_pairs):
        qp = q_ref[:, p * LANES:(p + 1) * LANES]
        qh += [jnp.where(low_half, qp, zero), jnp.where(low_half, zero, qp)]
    row = lax.broadcasted_iota(I32, (tq, tk), 0) + q0
    col0 = lax.broadcasted_iota(I32, (tq, tk), 1)
    upper = u_ref[...]

    c_ref[...] = jnp.zeros(c_ref.shape, F32)
    acc_ref[...] = jnp.zeros(acc_ref.shape, F32)

    def tile(t, masked):
        off = pl.multiple_of(t * tk, tk)
        strict = (col0 + off) < row
        ks = [k_ref[pl.ds(off, tk), p * LANES:(p + 1) * LANES] for p in range(n_pairs)]
        vs = [v_ref[pl.ds(off, tk), p * LANES:(p + 1) * LANES] for p in range(n_pairs)]
        zs = [_dot_nt(qh[h], ks[h // 2]) for h in range(N_HEADS_B)]
        log_hits, fails, sums = [], [], []
        for z in zs:
            neg_abs = pltpu.bitcast(pltpu.bitcast(z, jnp.uint32) | jnp.uint32(0x80000000), F32)
            soft = jnp.log2(1.0 + jnp.exp2(neg_abs))
            log_hit = jnp.minimum(z, 0.0) - soft
            log_fail = log_hit - z
            if masked:
                log_fail = jnp.where(strict, log_fail, 0.0)
            log_hits.append(log_hit)
            fails.append(log_fail.astype(BF16))
            sums.append(jnp.sum(log_fail, axis=1, keepdims=True))
        afters = [_dot(f, upper) for f in fails]
        probs = []
        for h in range(N_HEADS_B):
            a = jnp.exp2(log_hits[h] + (afters[h] + c_ref[h]))
            if masked:
                a = jnp.where(strict, a, 0.0)
            probs.append(a.astype(BF16))
        pvs = [_dot(probs[h], vs[h // 2]) for h in range(N_HEADS_B)]
        for p in range(n_pairs):
            acc_ref[p] += jnp.where(low_half, pvs[2 * p], pvs[2 * p + 1])
        for h in range(N_HEADS_B):
            c_ref[h] += sums[h]

    tile(n_t - 1, True)

    def body(j, carry):
        tile(n_t - 2 - j, False)
        return carry

    lax.fori_loop(0, n_t - 1, body, 0)
    for p in range(n_pairs):
        o_ref[:, p * LANES:(p + 1) * LANES] = acc_ref[p].astype(o_ref.dtype)


def _sb_attention(proj, *, tq=BLOCK_Q, tk=256):
    b, s, _ = proj.shape
    assert s % tk == 0 and s % tq == 0
    w = N_HEADS_B * HEAD_DIM
    upper = (np.arange(tk)[:, None] > np.arange(tk)[None, :]).astype(np.float32)
    resident = dict(pipeline_mode=pl.Buffered(1))
    return pl.pallas_call(
        functools.partial(_sb_kernel, tq=tq, tk=tk),
        grid=(b, s // tq),
        in_specs=[
            pl.BlockSpec((None, tq, w), lambda bi, i: (bi, i, COL_QB // w)),
            pl.BlockSpec((None, s, w), lambda bi, i: (bi, 0, COL_KB // w), **resident),
            pl.BlockSpec((None, s, w), lambda bi, i: (bi, 0, COL_VB // w), **resident),
            pl.BlockSpec((tk, tk), lambda bi, i: (0, 0)),
        ],
        out_specs=pl.BlockSpec((None, tq, w), lambda bi, i: (bi, i, 0)),
        out_shape=jax.ShapeDtypeStruct((b, s, w), BF16),
        scratch_shapes=[
            pltpu.VMEM((N_HEADS_B, tq, 1), F32),
            pltpu.VMEM((N_HEADS_B // 2, tq, LANES), F32),
        ],
        compiler_params=_params("parallel", "arbitrary"),
        name="stick_breaking_attention",
    )(proj, proj, proj, jnp.asarray(upper, BF16))


def _gmlp_tail_kernel(u_ref, v_ref, gv_ref, bv_ref, ws_ref, bs_ref, wo_ref, g_ref, h_ref, o_ref,
                      wt_ref, vn_ref, gated_ref, *, tm):
    @pl.when(pl.program_id(0) == 0)
    def _():
        r = lax.broadcasted_iota(I32, (CHUNK, CHUNK), 0)
        c = lax.broadcasted_iota(I32, (CHUNK, CHUNK), 1)
        for g in range(C_GROUPS):
            wt_ref[g] = jnp.where(c <= r, ws_ref[g], 0.0).astype(BF16)

    v = v_ref[...].astype(F32)
    mu = jnp.mean(v, axis=-1, keepdims=True)
    var = jnp.mean(jnp.square(v - mu), axis=-1, keepdims=True)
    vn_ref[...] = (((v - mu) * lax.rsqrt(var + EPS)) * gv_ref[...] + bv_ref[...]).astype(BF16)
    cw = C_WIDTH // C_GROUPS
    for ch in range(tm // CHUNK):
        rows = slice(ch * CHUNK, (ch + 1) * CHUNK)
        for g in range(C_GROUPS):
            cols = slice(g * cw, (g + 1) * cw)
            sv = _dot(wt_ref[g], vn_ref[rows, cols]) + bs_ref[g]
            gated_ref[rows, cols] = (u_ref[rows, cols].astype(F32) * sv).astype(BF16)
    y = _dot(gated_ref[...], wo_ref[...])
    o_ref[...] = h_ref[...] + _rms(y, g_ref[...])


def _gmlp_tail(z, gv, bv, ws, bs_b, wo, g, h, *, tm=256):
    t, d = h.shape
    cw = C_WIDTH // C_GROUPS
    return pl.pallas_call(
        functools.partial(_gmlp_tail_kernel, tm=tm),
        grid=(t // tm,),
        in_specs=[
            pl.BlockSpec((tm, C_WIDTH), lambda i: (i, 0)),
            pl.BlockSpec((tm, C_WIDTH), lambda i: (i, 1)),
            pl.BlockSpec((1, C_WIDTH), lambda i: (0, 0)),
            pl.BlockSpec((1, C_WIDTH), lambda i: (0, 0)),
            pl.BlockSpec((C_GROUPS, CHUNK, CHUNK), lambda i: (0, 0, 0)),
            pl.BlockSpec((C_GROUPS, CHUNK, cw), lambda i: (0, 0, 0)),
            pl.BlockSpec((C_WIDTH, d), lambda i: (0, 0)),
            pl.BlockSpec((1, d), lambda i: (0, 0)),
            pl.BlockSpec((tm, d), lambda i: (i, 0)),
        ],
        out_specs=pl.BlockSpec((tm, d), lambda i: (i, 0)),
        out_shape=jax.ShapeDtypeStruct((t, d), F32),
        scratch_shapes=[
            pltpu.VMEM((C_GROUPS, CHUNK, CHUNK), BF16),
            pltpu.VMEM((tm, C_WIDTH), BF16),
            pltpu.VMEM((tm, C_WIDTH), BF16),
        ],
        compiler_params=_params("arbitrary"),
        name="gmlp_tail",
    )(z, z, gv, bv, ws, bs_b, wo, g, h)


def _prep_even_weights(w):
    split_points = np.cumsum(EVEN_WIDTHS)[:-1].tolist()
    qa, ka, va, qi, ki, wi, qb, kb, vb = jnp.split(w, split_points, axis=1)
    d = w.shape[0]
    main = jnp.concatenate(
        [qa * (HEAD_DIM ** -0.5), qi * (IDX_DIM ** -0.5), ka, va, ki, jnp.zeros((d, LANES - IDX_DIM), w.dtype),
         qb * (HEAD_DIM ** -0.5 * LOG2_E), kb, vb], axis=1).astype(BF16)
    wi_cols = jnp.repeat(wi * (IDX_HEADS ** -0.5), LANES, axis=1).astype(BF16)
    return main, wi_cols


def kernel(x, norm_gains, w_in_even, w_out_even, w_in_odd, spatial_w, spatial_b, v_norm_gain, v_norm_bias,
           w_out_odd, w_ffn_in, w_ffn_out):
    b, s, d = x.shape
    t = b * s
    depth = norm_gains.shape[0]
    h = x.reshape(t, d)
    tm = 512
    mix_a = N_HEADS_A * HEAD_DIM
    for layer in range(depth):
        g = norm_gains[layer][:, None, :]
        i = layer // 2
        if layer % 2 == 0:
            w_main, w_wi = _prep_even_weights(w_in_even[i])
            proj = _norm_matmul(h, g[0], w_main, act="none", tm=tm, tn=EVEN_COLS // 2, out_dtype=BF16)
            wi = _norm_matmul(h, g[0], w_wi, act="none", tm=tm, tn=IDX_HEADS * LANES, out_dtype=F32)
            proj = proj.reshape(b, s, EVEN_COLS)
            ya = _dsa_attention(proj, wi.reshape(b, s, IDX_HEADS * LANES))
            yb = _sb_attention(proj)
            wo = w_out_even[i].astype(BF16)
            h = _mm_norm_res([ya.reshape(t, mix_a), yb.reshape(t, -1)], [wo[:mix_a], wo[mix_a:]], g[1], h, tm=tm)
        else:
            z = _norm_matmul(h, g[0], w_in_odd[i].astype(BF16), act="gelu", tm=tm, tn=1024, out_dtype=BF16)
            bs_b = jnp.broadcast_to(spatial_b[i][:, :, None], (C_GROUPS, CHUNK, C_WIDTH // C_GROUPS))
            h = _gmlp_tail(z, v_norm_gain[i][None, :], v_norm_bias[i][None, :], spatial_w[i], bs_b,
                           w_out_odd[i].astype(BF16), g[1], h)
        act = _norm_matmul(h, g[2], w_ffn_in[layer].astype(BF16), act="swiglu", tm=tm, tn=D_FF // 2,
                           out_dtype=BF16, n_out=D_FF)
        h = _mm_norm_res([act], [w_ffn_out[layer].astype(BF16)], g[3], h, tm=tm)
    return h.reshape(b, s, d)
```

```python
import functools

import numpy as np
import jax
import jax.numpy as jnp
from jax import lax
from jax.experimental import pallas as pl
from jax.experimental.pallas import tpu as pltpu

F32 = jnp.float32
BF16 = jnp.bfloat16
I32 = jnp.int32
I16 = jnp.int16

D_MODEL = 1024
HEAD_DIM = 64
N_HEADS_A = 8
N_HEADS_B = 8
IDX_HEADS = 4
IDX_DIM = 64
TOPK_MAX = 256
BLOCK_Q = 128
CHUNK = 128
C_WIDTH = 2048
C_GROUPS = 16
D_FF = 2816
EPS = 1e-6
EVEN_WIDTHS = (N_HEADS_A * HEAD_DIM, HEAD_DIM, HEAD_DIM, IDX_HEADS * IDX_DIM, IDX_DIM, IDX_HEADS,
               N_HEADS_B * HEAD_DIM, N_HEADS_B * HEAD_DIM, N_HEADS_B * HEAD_DIM)

LANES = 128
SUBLANES = 8
PACKED = 16
VMEM_LIMIT = 48 * 1024 * 1024

COL_QA = 0
COL_QI = 512
COL_KVA = 768
COL_KI = 896
COL_QB = 1024
COL_KB = 1536
COL_VB = 2048
EVEN_COLS = 2560

LOG2_E = float(np.log2(np.e))
MASKED = -1e30
NEG_INF_KEY = -2139095041


def _params(*sem):
    return pltpu.CompilerParams(dimension_semantics=sem, vmem_limit_bytes=VMEM_LIMIT)


def _dot(a, b):
    return jnp.dot(a, b, preferred_element_type=F32)


def _dot_nt(a, b):
    return lax.dot_general(a, b, (((1,), (1,)), ((), ())), preferred_element_type=F32)


def _rms(y, g):
    ms = jnp.mean(y * y, axis=-1, keepdims=True)
    return (y * lax.rsqrt(ms + EPS)) * g


def _gelu_tanh(x):
    c = np.float32(np.sqrt(2.0 / np.pi))
    return x * (0.5 * (1.0 + jnp.tanh(c * (x + 0.044715 * (x * x * x)))))


def _norm_matmul_kernel(x_ref, g_ref, *refs, act):
    n_w = 2 if act == "swiglu" else 1
    w_refs, o_ref, xn_ref = refs[:n_w], refs[n_w], refs[n_w + 1]

    @pl.when(pl.program_id(1) == 0)
    def _():
        xn_ref[...] = _rms(x_ref[...], g_ref[...]).astype(BF16)

    xn = xn_ref[...]
    y = _dot(xn, w_refs[0][...])
    if act == "gelu":
        y = _gelu_tanh(y)
    elif act == "swiglu":
        up = _dot(xn, w_refs[1][...])
        y = (y / (1.0 + jnp.exp(-y))) * up
    o_ref[...] = y.astype(o_ref.dtype)


def _norm_matmul(x, g, w, *, act, tm, tn, out_dtype, n_out=None):
    t, d = x.shape
    n_out = w.shape[1] if n_out is None else n_out
    in_specs = [pl.BlockSpec((tm, d), lambda i, j: (i, 0)),
                pl.BlockSpec((1, d), lambda i, j: (0, 0)),
                pl.BlockSpec((d, tn), lambda i, j: (0, j))]
    args = [x, g, w]
    if act == "swiglu":
        up_off = n_out // tn
        in_specs.append(pl.BlockSpec((d, tn), lambda i, j: (0, j + up_off)))
        args.append(w)
    return pl.pallas_call(
        functools.partial(_norm_matmul_kernel, act=act),
        grid=(t // tm, n_out // tn),
        in_specs=in_specs,
        out_specs=pl.BlockSpec((tm, tn), lambda i, j: (i, j)),
        out_shape=jax.ShapeDtypeStruct((t, n_out), out_dtype),
        scratch_shapes=[pltpu.VMEM((tm, d), BF16)],
        compiler_params=_params("parallel", "arbitrary"),
        name="norm_matmul_" + act,
    )(*args)


def _mm_norm_res_kernel(*refs, n_in):
    x_refs, w_refs = refs[:n_in], refs[n_in:2 * n_in]
    g_ref, h_ref, o_ref = refs[2 * n_in:]
    y = _dot(x_refs[0][...], w_refs[0][...])
    for k in range(1, n_in):
        y = y + _dot(x_refs[k][...], w_refs[k][...])
    o_ref[...] = h_ref[...] + _rms(y, g_ref[...])


def _mm_norm_res(xs, ws, g, h, *, tm):
    t, d = h.shape
    n_in = len(xs)
    in_specs = [pl.BlockSpec((tm, x.shape[1]), lambda i: (i, 0)) for x in xs]
    in_specs += [pl.BlockSpec(w.shape, lambda i: (0, 0)) for w in ws]
    in_specs += [pl.BlockSpec((1, d), lambda i: (0, 0)), pl.BlockSpec((tm, d), lambda i: (i, 0))]
    return pl.pallas_call(
        functools.partial(_mm_norm_res_kernel, n_in=n_in),
        grid=(t // tm,),
        in_specs=in_specs,
        out_specs=pl.BlockSpec((tm, d), lambda i: (i, 0)),
        out_shape=jax.ShapeDtypeStruct((t, d), F32),
        compiler_params=_params("parallel"),
        name="matmul_norm_residual",
    )(*xs, *ws, g, h)


def _dsa_kernel(q_ref, qi_ref, kv_ref, ki_ref, wi_ref, o_ref,
                key_ref, hi_ref, lo_ref, low_ref, tie_ref, kvt_ref, rq_ref, rqi_ref, w_ref, m_ref, acc_ref,
                *, tq, tk, topk, idx_bits, seq):
    i = pl.program_id(1)
    q0 = i * tq
    n_t = q0 // tk + 1
    half_rows = lax.broadcasted_iota(I32, (LANES, tq), 0) < HEAD_DIM

    @pl.when(i == 0)
    def _():
        def blk(j, carry):
            off = pl.multiple_of(j * LANES, LANES)
            t = kv_ref[pl.ds(off, LANES), :].astype(F32).T
            kvt_ref[:, pl.ds(off, LANES)] = jnp.where(half_rows, 1.0, t).astype(BF16)
            return carry
        lax.fori_loop(0, seq // LANES, blk, 0)

    zeros_half = jnp.zeros((HEAD_DIM, tq), F32)
    for pair in range(N_HEADS_A // 2):
        t = q_ref[:, pair * LANES:(pair + 1) * LANES].astype(F32).T
        even = jnp.where(half_rows, t, 0.0)
        odd = jnp.concatenate([t[HEAD_DIM:], zeros_half], axis=0)
        rq_ref[:, pl.ds(2 * pair * tq, tq)] = even.astype(BF16)
        rq_ref[:, pl.ds((2 * pair + 1) * tq, tq)] = odd.astype(BF16)
    for pair in range(IDX_HEADS // 2):
        t = qi_ref[:, pair * LANES:(pair + 1) * LANES].astype(F32).T
        swapped = jnp.concatenate([t[HEAD_DIM:], t[:HEAD_DIM]], axis=0)
        rqi_ref[:, pl.ds(2 * pair * tq, tq)] = t.astype(BF16)
        rqi_ref[:, pl.ds((2 * pair + 1) * tq, tq)] = swapped.astype(BF16)
    for h in range(IDX_HEADS):
        w_ref[h] = wi_ref[:, h * LANES:(h + 1) * LANES].T[:SUBLANES]

    qpos = lax.broadcasted_iota(I32, (tk, tq), 1) + q0
    krow = lax.broadcasted_iota(I32, (tk, tq), 0)

    def score_body(t, carry):
        off = pl.multiple_of(t * tk, tk)
        dots = _dot(ki_ref[pl.ds(off, tk), :], rqi_ref[...])
        sc = jnp.zeros((tk, tq), F32)
        for h in range(IDX_HEADS):
            sc = sc + jnp.maximum(dots[:, h * tq:(h + 1) * tq], 0.0) * w_ref[h][0:1]
        sc = jnp.where(krow + off <= qpos, sc, -jnp.inf)
        b = pltpu.bitcast(sc, I32)
        key = b ^ ((b >> 31) & 0x7FFFFFFF)
        key = jnp.where(key == -1, 0, key)
        key_ref[pl.ds(off, tk), :] = key
        hi_ref[pl.ds(off, tk), :] = (key >> 16).astype(I16)
        lo_ref[pl.ds(off, tk), :] = (key ^ 0x8000).astype(I16)
        return carry

    lax.fori_loop(0, n_t, score_body, 0)

    n_acc = 4
    one, zero = jnp.ones((PACKED, tq), BF16), jnp.zeros((PACKED, tq), BF16)

    def count(ref, pred):
        def body(t, accs):
            accs = list(accs)
            tile = ref[pl.ds(pl.multiple_of(t * tk, tk), tk), :]
            for r in range(tk // PACKED):
                x = tile[r * PACKED:(r + 1) * PACKED]
                accs[r % n_acc] = accs[r % n_acc] + jnp.where(pred(x), one, zero)
            return tuple(accs)
        accs = lax.fori_loop(0, n_t, body, (zero,) * n_acc)
        total = accs[0].astype(F32)
        for a in accs[1:]:
            total = total + a.astype(F32)
        return jnp.sum(total, axis=0, keepdims=True)

    def largest16(ref, want):
        v = jnp.where(count(ref, lambda x: x >= jnp.int16(0)) >= want, 0, -32768)
        v = jnp.broadcast_to(v, (PACKED, tq)).astype(I32)

        def body(p, v):
            cand = v | jnp.left_shift(jnp.int32(1), 14 - p)
            c16 = cand.astype(I16)
            return jnp.where(count(ref, lambda x: x >= c16) >= want, cand, v)

        return lax.fori_loop(0, 15, body, v)

    kf = np.float32(topk)
    thr_hi = largest16(hi_ref, kf)
    thr_hi16 = thr_hi.astype(I16)
    above = count(hi_ref, lambda x: x > thr_hi16)
    thr_hi_tile = jnp.broadcast_to(thr_hi16[0:1], (tk, tq))

    def low_body(t, carry):
        rows = pl.ds(pl.multiple_of(t * tk, tk), tk)
        low_ref[rows, :] = jnp.where(hi_ref[rows, :] == thr_hi_tile, lo_ref[rows, :], jnp.int16(-32768))
        return carry

    lax.fori_loop(0, n_t, low_body, 0)
    thr_lo = largest16(low_ref, kf - above)
    thr_lo16 = thr_lo.astype(I16)
    thr = (thr_hi[0:1] << 16) | ((thr_lo[0:1] + 32768) & 0xFFFF)

    need = kf - above - count(low_ref, lambda x: x > thr_lo16)
    none = jnp.int16(32767)
    dead = jnp.broadcast_to(jnp.where(thr == NEG_INF_KEY, 32767, 0), (tk, tq)).astype(I16)
    thr_lo_tile = jnp.broadcast_to(thr_lo16[0:1], (tk, tq))

    def tie_body(t, carry):
        off = pl.multiple_of(t * tk, tk)
        rows = pl.ds(off, tk)
        pos = (krow + off).astype(I16) | dead
        at_lo = jnp.where(lo_ref[rows, :] == thr_lo_tile, pos, none)
        tie_ref[rows, :] = jnp.where(hi_ref[rows, :] == thr_hi_tile, at_lo, none)
        return carry

    lax.fori_loop(0, n_t, tie_body, 0)

    def cut_body(p, cut):
        cand = cut | jnp.left_shift(jnp.int32(1), idx_bits - 1 - p)
        c16 = cand.astype(I16)
        return jnp.where(count(tie_ref, lambda x: x < c16) < need, cand, cut)

    cut = lax.fori_loop(0, idx_bits, cut_body, jnp.zeros((PACKED, tq), I32))
    cut_row = jnp.where(thr != NEG_INF_KEY, cut[0:1], -1)

    m_ref[...] = jnp.full(m_ref.shape, MASKED, F32)
    acc_ref[...] = jnp.zeros(acc_ref.shape, F32)

    def att_body(t, carry):
        off = pl.multiple_of(t * tk, tk)
        x = key_ref[pl.ds(off, tk), :]
        tied = jnp.where(krow + off <= cut_row, 0.0, MASKED)
        bias = jnp.where(x > thr, 0.0, jnp.where(x == thr, tied, MASKED))
        s_all = _dot(kv_ref[pl.ds(off, tk), :], rq_ref[...])
        probs, alphas = [], []
        for h in range(N_HEADS_A):
            s = s_all[:, h * tq:(h + 1) * tq] + bias
            m_old = m_ref[h]
            m_new = jnp.maximum(m_old, jnp.max(s, axis=0, keepdims=True))
            probs.append(jnp.exp(s - m_new).astype(BF16))
            alphas.append(jnp.exp(m_old - m_new))
            m_ref[h] = m_new
        pv = _dot(kvt_ref[:, pl.ds(off, tk)], jnp.concatenate(probs, axis=1))
        for h in range(N_HEADS_A):
            cols = pl.ds(h * tq, tq)
            acc_ref[:, cols] = alphas[h] * acc_ref[:, cols] + pv[:, h * tq:(h + 1) * tq]
        return carry

    lax.fori_loop(0, n_t, att_body, 0)

    for pair in range(N_HEADS_A // 2):
        outs = []
        for h in (2 * pair, 2 * pair + 1):
            a = acc_ref[:, pl.ds(h * tq, tq)]
            outs.append(a[HEAD_DIM:] / a[0:1])
        o_ref[:, pair * LANES:(pair + 1) * LANES] = jnp.concatenate(outs, axis=0).T.astype(o_ref.dtype)


def _dsa_attention(proj, wi, *, tq=BLOCK_Q, tk=512):
    b, s, _ = proj.shape
    topk = min(TOPK_MAX, s // 4)
    idx_bits = int(np.log2(s))
    assert 2 ** idx_bits == s and s < 2 ** 15 and s % tk == 0 and tk % tq == 0
    qa_w = N_HEADS_A * HEAD_DIM
    qi_w = IDX_HEADS * IDX_DIM
    return pl.pallas_call(
        functools.partial(_dsa_kernel, tq=tq, tk=tk, topk=topk, idx_bits=idx_bits, seq=s),
        grid=(b, s // tq),
        in_specs=[
            pl.BlockSpec((None, tq, qa_w), lambda bi, i: (bi, i, COL_QA // qa_w)),
            pl.BlockSpec((None, tq, qi_w), lambda bi, i: (bi, i, COL_QI // qi_w)),
            pl.BlockSpec((None, s, LANES), lambda bi, i: (bi, 0, COL_KVA // LANES)),
            pl.BlockSpec((None, s, LANES), lambda bi, i: (bi, 0, COL_KI // LANES)),
            pl.BlockSpec((None, tq, IDX_HEADS * LANES), lambda bi, i: (bi, i, 0)),
        ],
        out_specs=pl.BlockSpec((None, tq, qa_w), lambda bi, i: (bi, i, 0)),
        out_shape=jax.ShapeDtypeStruct((b, s, qa_w), BF16),
        scratch_shapes=[
            pltpu.VMEM((s, tq), I32),
            pltpu.VMEM((s, tq), I16),
            pltpu.VMEM((s, tq), I16),
            pltpu.VMEM((s, tq), I16),
            pltpu.VMEM((s, tq), I16),
            pltpu.VMEM((LANES, s), BF16),
            pltpu.VMEM((LANES, N_HEADS_A * tq), BF16),
            pltpu.VMEM((LANES, IDX_HEADS * tq), BF16),
            pltpu.VMEM((IDX_HEADS, SUBLANES, tq), F32),
            pltpu.VMEM((N_HEADS_A, 1, tq), F32),
            pltpu.VMEM((LANES, N_HEADS_A * tq), F32),
        ],
        compiler_params=_params("parallel", "arbitrary"),
        name="dsa_attention",
    )(proj, proj, proj, proj, wi)


def _sb_kernel(q_ref, k_ref, v_ref, u_ref, o_ref, c_ref, acc_ref, *, tq, tk):
    i = pl.program_id(1)
    q0 = i * tq
    n_t = (q0 + tq - 1) // tk + 1
    n_pairs = N_HEADS_B // 2
    lane = lax.broadcasted_iota(I32, (tq, LANES), 1)
    low_half = lane < HEAD_DIM
    zero = jnp.zeros((tq, LANES), BF16)
    qh = []
    for p in range(n_pairs):
        qp = q_ref[:, p * LANES:(p + 1) * LANES]
        qh += [jnp.where(low_half, qp, zero), jnp.where(low_half, zero, qp)]
    row = lax.broadcasted_iota(I32, (tq, tk), 0) + q0
    col0 = lax.broadcasted_iota(I32, (tq, tk), 1)
    upper = u_ref[...]

    c_ref[...] = jnp.zeros(c_ref.shape, F32)
    acc_ref[...] = jnp.zeros(acc_ref.shape, F32)

    def tile(t, masked):
        off = pl.multiple_of(t * tk, tk)
        strict = (col0 + off) < row
        ks = [k_ref[pl.ds(off, tk), p * LANES:(p + 1) * LANES] for p in range(n_pairs)]
        vs = [v_ref[pl.ds(off, tk), p * LANES:(p + 1) * LANES] for p in range(n_pairs)]
        zs = [_dot_nt(qh[h], ks[h // 2]) for h in range(N_HEADS_B)]
        log_hits, fails, sums = [], [], []
        for z in zs:
            neg_abs = pltpu.bitcast(pltpu.bitcast(z, jnp.uint32) | jnp.uint32(0x80000000), F32)
            soft = jnp.log2(1.0 + jnp.exp2(neg_abs))
            log_hit = jnp.minimum(z, 0.0) - soft
            log_fail = log_hit - z
            if masked:
                log_fail = jnp.where(strict, log_fail, 0.0)
            log_hits.append(log_hit)
            fails.append(log_fail.astype(BF16))
            sums.append(jnp.sum(log_fail, axis=1, keepdims=True))
        afters = [_dot(f, upper) for f in fails]
        probs = []
        for h in range(N_HEADS_B):
            a = jnp.exp2(log_hits[h] + (afters[h] + c_ref[h]))
            if masked:
                a = jnp.where(strict, a, 0.0)
            probs.append(a.astype(BF16))
        pvs = [_dot(probs[h], vs[h // 2]) for h in range(N_HEADS_B)]
        for p in range(n_pairs):
            acc_ref[p] += jnp.where(low_half, pvs[2 * p], pvs[2 * p + 1])
        for h in range(N_HEADS_B):
            c_ref[h] += sums[h]

    tile(n_t - 1, True)

    def body(j, carry):
        tile(n_t - 2 - j, False)
        return carry

    lax.fori_loop(0, n_t - 1, body, 0)
    for p in range(n_pairs):
        o_ref[:, p * LANES:(p + 1) * LANES] = acc_ref[p].astype(o_ref.dtype)


def _sb_attention(proj, *, tq=BLOCK_Q, tk=256):
    b, s, _ = proj.shape
    assert s % tk == 0 and s % tq == 0
    w = N_HEADS_B * HEAD_DIM
    upper = (np.arange(tk)[:, None] > np.arange(tk)[None, :]).astype(np.float32)
    resident = dict(pipeline_mode=pl.Buffered(1))
    return pl.pallas_call(
        functools.partial(_sb_kernel, tq=tq, tk=tk),
        grid=(b, s // tq),
        in_specs=[
            pl.BlockSpec((None, tq, w), lambda bi, i: (bi, i, COL_QB // w)),
            pl.BlockSpec((None, s, w), lambda bi, i: (bi, 0, COL_KB // w), **resident),
            pl.BlockSpec((None, s, w), lambda bi, i: (bi, 0, COL_VB // w), **resident),
            pl.BlockSpec((tk, tk), lambda bi, i: (0, 0)),
        ],
        out_specs=pl.BlockSpec((None, tq, w), lambda bi, i: (bi, i, 0)),
        out_shape=jax.ShapeDtypeStruct((b, s, w), BF16),
        scratch_shapes=[
            pltpu.VMEM((N_HEADS_B, tq, 1), F32),
            pltpu.VMEM((N_HEADS_B // 2, tq, LANES), F32),
        ],
        compiler_params=_params("parallel", "arbitrary"),
        name="stick_breaking_attention",
    )(proj, proj, proj, jnp.asarray(upper, BF16))


def _gmlp_tail_kernel(u_ref, v_ref, gv_ref, bv_ref, ws_ref, bs_ref, wo_ref, g_ref, h_ref, o_ref,
                      wt_ref, vn_ref, gated_ref, *, tm):
    @pl.when(pl.program_id(0) == 0)
    def _():
        r = lax.broadcasted_iota(I32, (CHUNK, CHUNK), 0)
        c = lax.broadcasted_iota(I32, (CHUNK, CHUNK), 1)
        for g in range(C_GROUPS):
            wt_ref[g] = jnp.where(c <= r, ws_ref[g], 0.0).astype(BF16)

    v = v_ref[...].astype(F32)
    mu = jnp.mean(v, axis=-1, keepdims=True)
    var = jnp.mean(jnp.square(v - mu), axis=-1, keepdims=True)
    vn_ref[...] = (((v - mu) * lax.rsqrt(var + EPS)) * gv_ref[...] + bv_ref[...]).astype(BF16)
    cw = C_WIDTH // C_GROUPS
    for ch in range(tm // CHUNK):
        rows = slice(ch * CHUNK, (ch + 1) * CHUNK)
        for g in range(C_GROUPS):
            cols = slice(g * cw, (g + 1) * cw)
            sv = _dot(wt_ref[g], vn_ref[rows, cols]) + bs_ref[g]
            gated_ref[rows, cols] = (u_ref[rows, cols].astype(F32) * sv).astype(BF16)
    y = _dot(gated_ref[...], wo_ref[...])
    o_ref[...] = h_ref[...] + _rms(y, g_ref[...])


def _gmlp_tail(z, gv, bv, ws, bs_b, wo, g, h, *, tm=256):
    t, d = h.shape
    cw = C_WIDTH // C_GROUPS
    return pl.pallas_call(
        functools.partial(_gmlp_tail_kernel, tm=tm),
        grid=(t // tm,),
        in_specs=[
            pl.BlockSpec((tm, C_WIDTH), lambda i: (i, 0)),
            pl.BlockSpec((tm, C_WIDTH), lambda i: (i, 1)),
            pl.BlockSpec((1, C_WIDTH), lambda i: (0, 0)),
            pl.BlockSpec((1, C_WIDTH), lambda i: (0, 0)),
            pl.BlockSpec((C_GROUPS, CHUNK, CHUNK), lambda i: (0, 0, 0)),
            pl.BlockSpec((C_GROUPS, CHUNK, cw), lambda i: (0, 0, 0)),
            pl.BlockSpec((C_WIDTH, d), lambda i: (0, 0)),
            pl.BlockSpec((1, d), lambda i: (0, 0)),
            pl.BlockSpec((tm, d), lambda i: (i, 0)),
        ],
        out_specs=pl.BlockSpec((tm, d), lambda i: (i, 0)),
        out_shape=jax.ShapeDtypeStruct((t, d), F32),
        scratch_shapes=[
            pltpu.VMEM((C_GROUPS, CHUNK, CHUNK), BF16),
            pltpu.VMEM((tm, C_WIDTH), BF16),
            pltpu.VMEM((tm, C_WIDTH), BF16),
        ],
        compiler_params=_params("arbitrary"),
        name="gmlp_tail",
    )(z, z, gv, bv, ws, bs_b, wo, g, h)


def _prep_even_weights(w):
    split_points = np.cumsum(EVEN_WIDTHS)[:-1].tolist()
    qa, ka, va, qi, ki, wi, qb, kb, vb = jnp.split(w, split_points, axis=1)
    d = w.shape[0]
    main = jnp.concatenate(
        [qa * (HEAD_DIM ** -0.5), qi * (IDX_DIM ** -0.5), ka, va, ki, jnp.zeros((d, LANES - IDX_DIM), w.dtype),
         qb * (HEAD_DIM ** -0.5 * LOG2_E), kb, vb], axis=1).astype(BF16)
    wi_cols = jnp.repeat(wi * (IDX_HEADS ** -0.5), LANES, axis=1).astype(BF16)
    return main, wi_cols


def kernel(x, norm_gains, w_in_even, w_out_even, w_in_odd, spatial_w, spatial_b, v_norm_gain, v_norm_bias,
           w_out_odd, w_ffn_in, w_ffn_out):
    b, s, d = x.shape
    t = b * s
    depth = norm_gains.shape[0]
    h = x.reshape(t, d)
    tm = 512
    mix_a = N_HEADS_A * HEAD_DIM
    for layer in range(depth):
        g = norm_gains[layer][:, None, :]
        i = layer // 2
        if layer % 2 == 0:
            w_main, w_wi = _prep_even_weights(w_in_even[i])
            proj = _norm_matmul(h, g[0], w_main, act="none", tm=tm, tn=EVEN_COLS // 2, out_dtype=BF16)
            wi = _norm_matmul(h, g[0], w_wi, act="none", tm=tm, tn=IDX_HEADS * LANES, out_dtype=F32)
            proj = proj.reshape(b, s, EVEN_COLS)
            ya = _dsa_attention(proj, wi.reshape(b, s, IDX_HEADS * LANES))
            yb = _sb_attention(proj)
            wo = w_out_even[i].astype(BF16)
            h = _mm_norm_res([ya.reshape(t, mix_a), yb.reshape(t, -1)], [wo[:mix_a], wo[mix_a:]], g[1], h, tm=tm)
        else:
            z = _norm_matmul(h, g[0], w_in_odd[i].astype(BF16), act="gelu", tm=tm, tn=1024, out_dtype=BF16)
            bs_b = jnp.broadcast_to(spatial_b[i][:, :, None], (C_GROUPS, CHUNK, C_WIDTH // C_GROUPS))
            h = _gmlp_tail(z, v_norm_gain[i][None, :], v_norm_bias[i][None, :], spatial_w[i], bs_b,
                           w_out_odd[i].astype(BF16), g[1], h)
        act = _norm_matmul(h, g[2], w_ffn_in[layer].astype(BF16), act="swiglu", tm=tm, tn=D_FF // 2,
                           out_dtype=BF16, n_out=D_FF)
        h = _mm_norm_res([act], [w_ffn_out[layer].astype(BF16)], g[3], h, tm=tm)
    return h.reshape(b, s, d)
```

```python
import functools

import numpy as np
import jax
import jax.numpy as jnp
from jax import lax
from jax.experimental import pallas as pl
from jax.experimental.pallas import tpu as pltpu

F32 = jnp.float32
BF16 = jnp.bfloat16
I32 = jnp.int32
I16 = jnp.int16

D_MODEL = 1024
HEAD_DIM = 64
N_HEADS_A = 8
N_HEADS_B = 8
IDX_HEADS = 4
IDX_DIM = 64
TOPK_MAX = 256
BLOCK_Q = 128
CHUNK = 128
C_WIDTH = 2048
C_GROUPS = 16
D_FF = 2816
EPS = 1e-6
EVEN_WIDTHS = (N_HEADS_A * HEAD_DIM, HEAD_DIM, HEAD_DIM, IDX_HEADS * IDX_DIM, IDX_DIM, IDX_HEADS,
               N_HEADS_B * HEAD_DIM, N_HEADS_B * HEAD_DIM, N_HEADS_B * HEAD_DIM)

LANES = 128
SUBLANES = 8
PACKED = 16
MAIN_TILES = 4
LOGIT_LIMIT = 40.0
VMEM_LIMIT = 48 * 1024 * 1024

COL_QA = 0
COL_QI = 512
COL_KVA = 768
COL_KI = 896
COL_QB = 1024
COL_KB = 1536
COL_VB = 2048
EVEN_COLS = 2560

LOG2_E = float(np.log2(np.e))
MASKED = -1e30
NEG_INF_KEY = -2139095041


def _params(*sem):
    return pltpu.CompilerParams(dimension_semantics=sem, vmem_limit_bytes=VMEM_LIMIT)


def _dot(a, b):
    return jnp.dot(a, b, preferred_element_type=F32)


def _dot_nt(a, b):
    return lax.dot_general(a, b, (((1,), (1,)), ((), ())), preferred_element_type=F32)


def _rms(y, g):
    ms = jnp.mean(y * y, axis=-1, keepdims=True)
    return (y * lax.rsqrt(ms + EPS)) * g


def _gelu_tanh(x):
    c = np.float32(np.sqrt(2.0 / np.pi))
    return x * (0.5 * (1.0 + jnp.tanh(c * (x + 0.044715 * (x * x * x)))))


def _norm_matmul_kernel(x_ref, g_ref, *refs, act):
    n_w = 2 if act == "swiglu" else 1
    w_refs, o_ref, xn_ref = refs[:n_w], refs[n_w], refs[n_w + 1]

    @pl.when(pl.program_id(1) == 0)
    def _():
        xn_ref[...] = _rms(x_ref[...], g_ref[...]).astype(BF16)

    xn = xn_ref[...]
    y = _dot(xn, w_refs[0][...])
    if act == "gelu":
        y = _gelu_tanh(y)
    elif act == "swiglu":
        up = _dot(xn, w_refs[1][...])
        y = (y / (1.0 + jnp.exp(-y))) * up
    o_ref[...] = y.astype(o_ref.dtype)


def _norm_matmul(x, g, w, *, act, tm, tn, out_dtype, n_out=None):
    t, d = x.shape
    n_out = w.shape[1] if n_out is None else n_out
    in_specs = [pl.BlockSpec((tm, d), lambda i, j: (i, 0)),
                pl.BlockSpec((1, d), lambda i, j: (0, 0)),
                pl.BlockSpec((d, tn), lambda i, j: (0, j))]
    args = [x, g, w]
    if act == "swiglu":
        up_off = n_out // tn
        in_specs.append(pl.BlockSpec((d, tn), lambda i, j: (0, j + up_off)))
        args.append(w)
    return pl.pallas_call(
        functools.partial(_norm_matmul_kernel, act=act),
        grid=(t // tm, n_out // tn),
        in_specs=in_specs,
        out_specs=pl.BlockSpec((tm, tn), lambda i, j: (i, j)),
        out_shape=jax.ShapeDtypeStruct((t, n_out), out_dtype),
        scratch_shapes=[pltpu.VMEM((tm, d), BF16)],
        compiler_params=_params("parallel", "arbitrary"),
        name="norm_matmul_" + act,
    )(*args)


def _mm_norm_res_kernel(*refs, n_in):
    x_refs, w_refs = refs[:n_in], refs[n_in:2 * n_in]
    g_ref, h_ref, o_ref = refs[2 * n_in:]
    y = _dot(x_refs[0][...], w_refs[0][...])
    for k in range(1, n_in):
        y = y + _dot(x_refs[k][...], w_refs[k][...])
    o_ref[...] = h_ref[...] + _rms(y, g_ref[...])


def _mm_norm_res(xs, ws, g, h, *, tm):
    t, d = h.shape
    n_in = len(xs)
    in_specs = [pl.BlockSpec((tm, x.shape[1]), lambda i: (i, 0)) for x in xs]
    in_specs += [pl.BlockSpec(w.shape, lambda i: (0, 0)) for w in ws]
    in_specs += [pl.BlockSpec((1, d), lambda i: (0, 0)), pl.BlockSpec((tm, d), lambda i: (i, 0))]
    return pl.pallas_call(
        functools.partial(_mm_norm_res_kernel, n_in=n_in),
        grid=(t // tm,),
        in_specs=in_specs,
        out_specs=pl.BlockSpec((tm, d), lambda i: (i, 0)),
        out_shape=jax.ShapeDtypeStruct((t, d), F32),
        compiler_params=_params("parallel"),
        name="matmul_norm_residual",
    )(*xs, *ws, g, h)


def _dsa_kernel(q_ref, qi_ref, kv_ref, ki_ref, wi_ref, o_ref,
                key_ref, hi_ref, lo_ref, low_ref, tie_ref, kvt_ref, rq_ref, rqi_ref, w_ref, m_ref, acc_ref, kmax_ref,
                *, tq, tk, topk, idx_bits, seq):
    i = pl.program_id(1)
    q0 = i * tq
    n_t = q0 // tk + 1
    half_rows = lax.broadcasted_iota(I32, (LANES, tq), 0) < HEAD_DIM

    @pl.when(i == 0)
    def _():
        feat_is_k = lax.broadcasted_iota(I32, (LANES, LANES), 0) < HEAD_DIM

        def blk(j, kmax2):
            off = pl.multiple_of(j * LANES, LANES)
            t = kv_ref[pl.ds(off, LANES), :].astype(F32).T
            kvt_ref[:, pl.ds(off, LANES)] = jnp.where(feat_is_k, 1.0, t).astype(BF16)
            return jnp.maximum(kmax2, jnp.sum(jnp.where(feat_is_k, t * t, 0.0), axis=0, keepdims=True))

        kmax2 = lax.fori_loop(0, seq // LANES, blk, jnp.zeros((1, LANES), F32))
        kmax_ref[...] = jnp.broadcast_to(jnp.max(kmax2, axis=1, keepdims=True), (1, LANES))

    zeros_half = jnp.zeros((HEAD_DIM, tq), F32)
    q2 = jnp.zeros((1, tq), F32)
    for pair in range(N_HEADS_A // 2):
        t = q_ref[:, pair * LANES:(pair + 1) * LANES].astype(F32).T
        even = jnp.where(half_rows, t, 0.0)
        odd = jnp.concatenate([t[HEAD_DIM:], zeros_half], axis=0)
        rq_ref[:, pl.ds(2 * pair * tq, tq)] = even.astype(BF16)
        rq_ref[:, pl.ds((2 * pair + 1) * tq, tq)] = odd.astype(BF16)
        q2 = jnp.maximum(q2, jnp.sum(t * t, axis=0, keepdims=True))
    for pair in range(IDX_HEADS // 2):
        t = qi_ref[:, pair * LANES:(pair + 1) * LANES].astype(F32).T
        swapped = jnp.concatenate([t[HEAD_DIM:], t[:HEAD_DIM]], axis=0)
        rqi_ref[:, pl.ds(2 * pair * tq, tq)] = t.astype(BF16)
        rqi_ref[:, pl.ds((2 * pair + 1) * tq, tq)] = swapped.astype(BF16)
    for h in range(IDX_HEADS):
        w_ref[h] = wi_ref[:, h * LANES:(h + 1) * LANES].T[:SUBLANES]

    qpos = lax.broadcasted_iota(I32, (tk, tq), 1) + q0
    krow = lax.broadcasted_iota(I32, (tk, tq), 0)

    def score_body(t, carry):
        off = pl.multiple_of(t * tk, tk)
        dots = _dot(ki_ref[pl.ds(off, tk), :], rqi_ref[...])
        sc = jnp.zeros((tk, tq), F32)
        for h in range(IDX_HEADS):
            sc = sc + jnp.maximum(dots[:, h * tq:(h + 1) * tq], 0.0) * w_ref[h][0:1]
        sc = jnp.where(krow + off <= qpos, sc, -jnp.inf)
        b = pltpu.bitcast(sc, I32)
        key = b ^ ((b >> 31) & 0x7FFFFFFF)
        key = jnp.where(key == -1, 0, key)
        key_ref[pl.ds(off, tk), :] = key
        hi_ref[pl.ds(off, tk), :] = (key >> 16).astype(I16)
        lo_ref[pl.ds(off, tk), :] = (key ^ 0x8000).astype(I16)
        return carry

    lax.fori_loop(0, n_t, score_body, 0)

    n_acc = 4
    one, zero = jnp.ones((PACKED, tq), BF16), jnp.zeros((PACKED, tq), BF16)

    n_main = n_t // MAIN_TILES
    n_tail = n_t - n_main * MAIN_TILES

    def count(ref, pred):
        def body(width, base, t, accs):
            accs = list(accs)
            tile = ref[pl.ds(pl.multiple_of(base + t * width, tk), width), :]
            for r in range(width // PACKED):
                x = tile[r * PACKED:(r + 1) * PACKED]
                accs[r % n_acc] = accs[r % n_acc] + jnp.where(pred(x), one, zero)
            return tuple(accs)
        accs = lax.fori_loop(0, n_main, functools.partial(body, MAIN_TILES * tk, 0), (zero,) * n_acc)
        accs = lax.fori_loop(0, n_tail, functools.partial(body, tk, n_main * MAIN_TILES * tk), accs)
        total = accs[0].astype(F32)
        for a in accs[1:]:
            total = total + a.astype(F32)
        return jnp.sum(total, axis=0, keepdims=True)

    def largest16(ref, want):
        c = count(ref, lambda x: x >= jnp.int16(0))
        ok = c >= want
        v = jnp.broadcast_to(jnp.where(ok, 0, -32768), (PACKED, tq)).astype(I32)
        above = jnp.where(ok, 0.0, c)

        def body(p, carry):
            v, above = carry
            cand = v | jnp.left_shift(jnp.int32(1), 14 - p)
            c16 = cand.astype(I16)
            c = count(ref, lambda x: x >= c16)
            ok = c >= want
            return jnp.where(ok, cand, v), jnp.where(ok, above, c)

        return lax.fori_loop(0, 15, body, (v, above))

    kf = np.float32(topk)
    thr_hi, above_hi = largest16(hi_ref, kf)
    thr_hi16 = thr_hi.astype(I16)
    thr_hi_tile = jnp.broadcast_to(thr_hi16[0:1], (tk, tq))

    def low_body(t, carry):
        rows = pl.ds(pl.multiple_of(t * tk, tk), tk)
        low_ref[rows, :] = jnp.where(hi_ref[rows, :] == thr_hi_tile, lo_ref[rows, :], jnp.int16(-32768))
        return carry

    lax.fori_loop(0, n_t, low_body, 0)
    thr_lo, above_lo = largest16(low_ref, kf - above_hi)
    thr_lo16 = thr_lo.astype(I16)
    thr = (thr_hi[0:1] << 16) | ((thr_lo[0:1] + 32768) & 0xFFFF)

    need = kf - above_hi - above_lo
    none = jnp.int16(32767)
    dead = jnp.broadcast_to(jnp.where(thr == NEG_INF_KEY, 32767, 0), (tk, tq)).astype(I16)
    thr_lo_tile = jnp.broadcast_to(thr_lo16[0:1], (tk, tq))

    def tie_body(t, carry):
        off = pl.multiple_of(t * tk, tk)
        rows = pl.ds(off, tk)
        pos = (krow + off).astype(I16) | dead
        at_lo = jnp.where(lo_ref[rows, :] == thr_lo_tile, pos, none)
        tie_ref[rows, :] = jnp.where(hi_ref[rows, :] == thr_hi_tile, at_lo, none)
        return carry

    lax.fori_loop(0, n_t, tie_body, 0)

    def cut_body(p, cut):
        cand = cut | jnp.left_shift(jnp.int32(1), idx_bits - 1 - p)
        c16 = cand.astype(I16)
        return jnp.where(count(tie_ref, lambda x: x < c16) < need, cand, cut)

    cut = lax.fori_loop(0, idx_bits, cut_body, jnp.zeros((PACKED, tq), I32))
    cut_row = jnp.where(thr != NEG_INF_KEY, cut[0:1], -1)

    acc_ref[...] = jnp.zeros(acc_ref.shape, F32)
    bound2 = jnp.max(q2, axis=1, keepdims=True) * kmax_ref[:, 0:1]
    safe = jnp.max(bound2) <= LOGIT_LIMIT * LOGIT_LIMIT

    @pl.when(safe)
    def _():
        one_t, zero_t = jnp.ones((tk, tq), BF16), jnp.zeros((tk, tq), BF16)
        cut_tile = jnp.broadcast_to(cut_row, (tk, tq)).astype(I16)

        def body(t, carry):
            rows = pl.ds(pl.multiple_of(t * tk, tk), tk)
            tied = jnp.where(tie_ref[rows, :] <= cut_tile, one_t, zero_t)
            keep = jnp.where(hi_ref[rows, :] > thr_hi_tile, one_t,
                             jnp.where(low_ref[rows, :] > thr_lo_tile, one_t, tied))
            s_all = _dot(kv_ref[rows, :], rq_ref[...])
            probs = [jnp.exp(s_all[:, h * tq:(h + 1) * tq]).astype(BF16) * keep for h in range(N_HEADS_A)]
            acc_ref[...] += _dot(kvt_ref[:, rows], jnp.concatenate(probs, axis=1))
            return carry

        lax.fori_loop(0, n_t, body, 0)

    @pl.when(jnp.logical_not(safe))
    def _():
        m_ref[...] = jnp.full(m_ref.shape, MASKED, F32)

        def body(t, carry):
            off = pl.multiple_of(t * tk, tk)
            x = key_ref[pl.ds(off, tk), :]
            tied = jnp.where(krow + off <= cut_row, 0.0, MASKED)
            bias = jnp.where(x > thr, 0.0, jnp.where(x == thr, tied, MASKED))
            s_all = _dot(kv_ref[pl.ds(off, tk), :], rq_ref[...])
            probs, alphas = [], []
            for h in range(N_HEADS_A):
                s = s_all[:, h * tq:(h + 1) * tq] + bias
                m_old = m_ref[h]
                m_new = jnp.maximum(m_old, jnp.max(s, axis=0, keepdims=True))
                probs.append(jnp.exp(s - m_new).astype(BF16))
                alphas.append(jnp.exp(m_old - m_new))
                m_ref[h] = m_new
            pv = _dot(kvt_ref[:, pl.ds(off, tk)], jnp.concatenate(probs, axis=1))
            for h in range(N_HEADS_A):
                cols = pl.ds(h * tq, tq)
                acc_ref[:, cols] = alphas[h] * acc_ref[:, cols] + pv[:, h * tq:(h + 1) * tq]
            return carry

        lax.fori_loop(0, n_t, body, 0)

    for pair in range(N_HEADS_A // 2):
        outs = []
        for h in (2 * pair, 2 * pair + 1):
            a = acc_ref[:, pl.ds(h * tq, tq)]
            outs.append(a[HEAD_DIM:] / a[0:1])
        o_ref[:, pair * LANES:(pair + 1) * LANES] = jnp.concatenate(outs, axis=0).T.astype(o_ref.dtype)


def _dsa_attention(proj, wi, *, tq=BLOCK_Q, tk=512):
    b, s, _ = proj.shape
    topk = min(TOPK_MAX, s // 4)
    idx_bits = int(np.log2(s))
    assert 2 ** idx_bits == s and s < 2 ** 15 and s % tk == 0 and tk % tq == 0
    qa_w = N_HEADS_A * HEAD_DIM
    qi_w = IDX_HEADS * IDX_DIM
    return pl.pallas_call(
        functools.partial(_dsa_kernel, tq=tq, tk=tk, topk=topk, idx_bits=idx_bits, seq=s),
        grid=(b, s // tq),
        in_specs=[
            pl.BlockSpec((None, tq, qa_w), lambda bi, i: (bi, i, COL_QA // qa_w)),
            pl.BlockSpec((None, tq, qi_w), lambda bi, i: (bi, i, COL_QI // qi_w)),
            pl.BlockSpec((None, s, LANES), lambda bi, i: (bi, 0, COL_KVA // LANES)),
            pl.BlockSpec((None, s, LANES), lambda bi, i: (bi, 0, COL_KI // LANES)),
            pl.BlockSpec((None, tq, IDX_HEADS * LANES), lambda bi, i: (bi, i, 0)),
        ],
        out_specs=pl.BlockSpec((None, tq, qa_w), lambda bi, i: (bi, i, 0)),
        out_shape=jax.ShapeDtypeStruct((b, s, qa_w), BF16),
        scratch_shapes=[
            pltpu.VMEM((s, tq), I32),
            pltpu.VMEM((s, tq), I16),
            pltpu.VMEM((s, tq), I16),
            pltpu.VMEM((s, tq), I16),
            pltpu.VMEM((s, tq), I16),
            pltpu.VMEM((LANES, s), BF16),
            pltpu.VMEM((LANES, N_HEADS_A * tq), BF16),
            pltpu.VMEM((LANES, IDX_HEADS * tq), BF16),
            pltpu.VMEM((IDX_HEADS, SUBLANES, tq), F32),
            pltpu.VMEM((N_HEADS_A, 1, tq), F32),
            pltpu.VMEM((LANES, N_HEADS_A * tq), F32),
            pltpu.VMEM((1, LANES), F32),
        ],
        compiler_params=_params("parallel", "arbitrary"),
        name="dsa_attention",
    )(proj, proj, proj, proj, wi)


def _sb_kernel(q_ref, k_ref, v_ref, u_ref, o_ref, c_ref, acc_ref, *, tq, tk):
    i = pl.program_id(1)
    q0 = i * tq
    n_t = (q0 + tq - 1) // tk + 1
    n_pairs = N_HEADS_B // 2
    lane = lax.broadcasted_iota(I32, (tq, LANES), 1)
    low_half = lane < HEAD_DIM
    zero = jnp.zeros((tq, LANES), BF16)
    qh = []
    for p in range(n_pairs):
        qp = q_ref[:, p * LANES:(p + 1) * LANES]
        qh += [jnp.where(low_half, qp, zero), jnp.where(low_half, zero, qp)]
    row = lax.broadcasted_iota(I32, (tq, tk), 0) + q0
    col0 = lax.broadcasted_iota(I32, (tq, tk), 1)
    upper = u_ref[...]

    c_ref[...] = jnp.zeros(c_ref.shape, F32)
    acc_ref[...] = jnp.zeros(acc_ref.shape, F32)

    def tile(t, masked):
        off = pl.multiple_of(t * tk, tk)
        strict = (col0 + off) < row
        ks = [k_ref[pl.ds(off, tk), p * LANES:(p + 1) * LANES] for p in range(n_pairs)]
        vs = [v_ref[pl.ds(off, tk), p * LANES:(p + 1) * LANES] for p in range(n_pairs)]
        zs = [_dot_nt(qh[h], ks[h // 2]) for h in range(N_HEADS_B)]
        log_hits, fails, sums = [], [], []
        for z in zs:
            neg_abs = pltpu.bitcast(pltpu.bitcast(z, jnp.uint32) | jnp.uint32(0x80000000), F32)
            soft = jnp.log2(1.0 + jnp.exp2(neg_abs))
            log_hit = jnp.minimum(z, 0.0) - soft
            log_fail = log_hit - z
            if masked:
                log_fail = jnp.where(strict, log_fail, 0.0)
            log_hits.append(log_hit)
            fails.append(log_fail.astype(BF16))
            sums.append(jnp.sum(log_fail, axis=1, keepdims=True))
        afters = [_dot(f, upper) for f in fails]
        probs = []
        for h in range(N_HEADS_B):
            a = jnp.exp2(log_hits[h] + (afters[h] + c_ref[h]))
            if masked:
                a = jnp.where(strict, a, 0.0)
            probs.append(a.astype(BF16))
        pvs = [_dot(probs[h], vs[h // 2]) for h in range(N_HEADS_B)]
        for p in range(n_pairs):
            acc_ref[p] += jnp.where(low_half, pvs[2 * p], pvs[2 * p + 1])
        for h in range(N_HEADS_B):
            c_ref[h] += sums[h]

    tile(n_t - 1, True)

    def body(j, carry):
        tile(n_t - 2 - j, False)
        return carry

    lax.fori_loop(0, n_t - 1, body, 0)
    for p in range(n_pairs):
        o_ref[:, p * LANES:(p + 1) * LANES] = acc_ref[p].astype(o_ref.dtype)


def _sb_attention(proj, *, tq=BLOCK_Q, tk=256):
    b, s, _ = proj.shape
    assert s % tk == 0 and s % tq == 0
    w = N_HEADS_B * HEAD_DIM
    upper = (np.arange(tk)[:, None] > np.arange(tk)[None, :]).astype(np.float32)
    resident = dict(pipeline_mode=pl.Buffered(1))
    return pl.pallas_call(
        functools.partial(_sb_kernel, tq=tq, tk=tk),
        grid=(b, s // tq),
        in_specs=[
            pl.BlockSpec((None, tq, w), lambda bi, i: (bi, i, COL_QB // w)),
            pl.BlockSpec((None, s, w), lambda bi, i: (bi, 0, COL_KB // w), **resident),
            pl.BlockSpec((None, s, w), lambda bi, i: (bi, 0, COL_VB // w), **resident),
            pl.BlockSpec((tk, tk), lambda bi, i: (0, 0)),
        ],
        out_specs=pl.BlockSpec((None, tq, w), lambda bi, i: (bi, i, 0)),
        out_shape=jax.ShapeDtypeStruct((b, s, w), BF16),
        scratch_shapes=[
            pltpu.VMEM((N_HEADS_B, tq, 1), F32),
            pltpu.VMEM((N_HEADS_B // 2, tq, LANES), F32),
        ],
        compiler_params=_params("parallel", "arbitrary"),
        name="stick_breaking_attention",
    )(proj, proj, proj, jnp.asarray(upper, BF16))


def _gmlp_tail_kernel(u_ref, v_ref, gv_ref, bv_ref, ws_ref, bs_ref, wo_ref, g_ref, h_ref, o_ref,
                      wt_ref, vn_ref, gated_ref, *, tm):
    @pl.when(pl.program_id(0) == 0)
    def _():
        r = lax.broadcasted_iota(I32, (CHUNK, CHUNK), 0)
        c = lax.broadcasted_iota(I32, (CHUNK, CHUNK), 1)
        for g in range(C_GROUPS):
            wt_ref[g] = jnp.where(c <= r, ws_ref[g], 0.0).astype(BF16)

    v = v_ref[...].astype(F32)
    mu = jnp.mean(v, axis=-1, keepdims=True)
    var = jnp.mean(jnp.square(v - mu), axis=-1, keepdims=True)
    vn_ref[...] = (((v - mu) * lax.rsqrt(var + EPS)) * gv_ref[...] + bv_ref[...]).astype(BF16)
    cw = C_WIDTH // C_GROUPS
    for ch in range(tm // CHUNK):
        rows = slice(ch * CHUNK, (ch + 1) * CHUNK)
        for g in range(C_GROUPS):
            cols = slice(g * cw, (g + 1) * cw)
            sv = _dot(wt_ref[g], vn_ref[rows, cols]) + bs_ref[g]
            gated_ref[rows, cols] = (u_ref[rows, cols].astype(F32) * sv).astype(BF16)
    y = _dot(gated_ref[...], wo_ref[...])
    o_ref[...] = h_ref[...] + _rms(y, g_ref[...])


def _gmlp_tail(z, gv, bv, ws, bs_b, wo, g, h, *, tm=256):
    t, d = h.shape
    cw = C_WIDTH // C_GROUPS
    return pl.pallas_call(
        functools.partial(_gmlp_tail_kernel, tm=tm),
        grid=(t // tm,),
        in_specs=[
            pl.BlockSpec((tm, C_WIDTH), lambda i: (i, 0)),
            pl.BlockSpec((tm, C_WIDTH), lambda i: (i, 1)),
            pl.BlockSpec((1, C_WIDTH), lambda i: (0, 0)),
            pl.BlockSpec((1, C_WIDTH), lambda i: (0, 0)),
            pl.BlockSpec((C_GROUPS, CHUNK, CHUNK), lambda i: (0, 0, 0)),
            pl.BlockSpec((C_GROUPS, CHUNK, cw), lambda i: (0, 0, 0)),
            pl.BlockSpec((C_WIDTH, d), lambda i: (0, 0)),
            pl.BlockSpec((1, d), lambda i: (0, 0)),
            pl.BlockSpec((tm, d), lambda i: (i, 0)),
        ],
        out_specs=pl.BlockSpec((tm, d), lambda i: (i, 0)),
        out_shape=jax.ShapeDtypeStruct((t, d), F32),
        scratch_shapes=[
            pltpu.VMEM((C_GROUPS, CHUNK, CHUNK), BF16),
            pltpu.VMEM((tm, C_WIDTH), BF16),
            pltpu.VMEM((tm, C_WIDTH), BF16),
        ],
        compiler_params=_params("arbitrary"),
        name="gmlp_tail",
    )(z, z, gv, bv, ws, bs_b, wo, g, h)


def _prep_even_weights(w):
    split_points = np.cumsum(EVEN_WIDTHS)[:-1].tolist()
    qa, ka, va, qi, ki, wi, qb, kb, vb = jnp.split(w, split_points, axis=1)
    d = w.shape[0]
    main = jnp.concatenate(
        [qa * (HEAD_DIM ** -0.5), qi * (IDX_DIM ** -0.5), ka, va, ki, jnp.zeros((d, LANES - IDX_DIM), w.dtype),
         qb * (HEAD_DIM ** -0.5 * LOG2_E), kb, vb], axis=1).astype(BF16)
    wi_cols = jnp.repeat(wi * (IDX_HEADS ** -0.5), LANES, axis=1).astype(BF16)
    return main, wi_cols


def kernel(x, norm_gains, w_in_even, w_out_even, w_in_odd, spatial_w, spatial_b, v_norm_gain, v_norm_bias,
           w_out_odd, w_ffn_in, w_ffn_out):
    b, s, d = x.shape
    t = b * s
    depth = norm_gains.shape[0]
    h = x.reshape(t, d)
    tm = 512
    mix_a = N_HEADS_A * HEAD_DIM
    for layer in range(depth):
        g = norm_gains[layer][:, None, :]
        i = layer // 2
        if layer % 2 == 0:
            w_main, w_wi = _prep_even_weights(w_in_even[i])
            proj = _norm_matmul(h, g[0], w_main, act="none", tm=tm, tn=EVEN_COLS // 2, out_dtype=BF16)
            wi = _norm_matmul(h, g[0], w_wi, act="none", tm=tm, tn=IDX_HEADS * LANES, out_dtype=F32)
            proj = proj.reshape(b, s, EVEN_COLS)
            ya = _dsa_attention(proj, wi.reshape(b, s, IDX_HEADS * LANES))
            yb = _sb_attention(proj)
            wo = w_out_even[i].astype(BF16)
            h = _mm_norm_res([ya.reshape(t, mix_a), yb.reshape(t, -1)], [wo[:mix_a], wo[mix_a:]], g[1], h, tm=tm)
        else:
            z = _norm_matmul(h, g[0], w_in_odd[i].astype(BF16), act="gelu", tm=tm, tn=1024, out_dtype=BF16)
            bs_b = jnp.broadcast_to(spatial_b[i][:, :, None], (C_GROUPS, CHUNK, C_WIDTH // C_GROUPS))
            h = _gmlp_tail(z, v_norm_gain[i][None, :], v_norm_bias[i][None, :], spatial_w[i], bs_b,
                           w_out_odd[i].astype(BF16), g[1], h)
        act = _norm_matmul(h, g[2], w_ffn_in[layer].astype(BF16), act="swiglu", tm=tm, tn=D_FF // 2,
                           out_dtype=BF16, n_out=D_FF)
        h = _mm_norm_res([act], [w_ffn_out[layer].astype(BF16)], g[3], h, tm=tm)
    return h.reshape(b, s, d)
```

```python
import functools

import numpy as np
import jax
import jax.numpy as jnp
from jax import lax
from jax.experimental import pallas as pl
from jax.experimental.pallas import tpu as pltpu

F32 = jnp.float32
BF16 = jnp.bfloat16
I32 = jnp.int32
I16 = jnp.int16

D_MODEL = 1024
HEAD_DIM = 64
N_HEADS_A = 8
N_HEADS_B = 8
IDX_HEADS = 4
IDX_DIM = 64
TOPK_MAX = 256
BLOCK_Q = 128
CHUNK = 128
C_WIDTH = 2048
C_GROUPS = 16
D_FF = 2816
EPS = 1e-6
EVEN_WIDTHS = (N_HEADS_A * HEAD_DIM, HEAD_DIM, HEAD_DIM, IDX_HEADS * IDX_DIM, IDX_DIM, IDX_HEADS,
               N_HEADS_B * HEAD_DIM, N_HEADS_B * HEAD_DIM, N_HEADS_B * HEAD_DIM)

LANES = 128
SUBLANES = 8
PACKED = 16
MAIN_TILES = 4
LOGIT_LIMIT = 40.0
VMEM_LIMIT = 48 * 1024 * 1024

COL_QA = 0
COL_QI = 512
COL_KVA = 768
COL_KI = 896
COL_QB = 1024
COL_KB = 1536
COL_VB = 2048
EVEN_COLS = 2560

LOG2_E = float(np.log2(np.e))
DEAD_LOG2 = -160.0
MASKED = -1e30
NEG_INF_KEY = -2139095041


def _params(*sem):
    return pltpu.CompilerParams(dimension_semantics=sem, vmem_limit_bytes=VMEM_LIMIT)


def _dot(a, b):
    return jnp.dot(a, b, preferred_element_type=F32)


def _dot_nt(a, b):
    return lax.dot_general(a, b, (((1,), (1,)), ((), ())), preferred_element_type=F32)


def _rms(y, g):
    ms = jnp.mean(y * y, axis=-1, keepdims=True)
    return (y * lax.rsqrt(ms + EPS)) * g


def _gelu_tanh(x):
    c = np.float32(np.sqrt(2.0 / np.pi))
    return x * (0.5 * (1.0 + jnp.tanh(c * (x + 0.044715 * (x * x * x)))))


def _norm_matmul_kernel(x_ref, g_ref, *refs, act):
    n_w = 2 if act == "swiglu" else 1
    w_refs, o_ref, xn_ref = refs[:n_w], refs[n_w], refs[n_w + 1]

    @pl.when(pl.program_id(1) == 0)
    def _():
        xn_ref[...] = _rms(x_ref[...], g_ref[...]).astype(BF16)

    xn = xn_ref[...]
    y = _dot(xn, w_refs[0][...])
    if act == "gelu":
        y = _gelu_tanh(y)
    elif act == "swiglu":
        up = _dot(xn, w_refs[1][...])
        y = (y / (1.0 + jnp.exp(-y))) * up
    o_ref[...] = y.astype(o_ref.dtype)


def _norm_matmul(x, g, w, *, act, tm, tn, out_dtype, n_out=None):
    t, d = x.shape
    n_out = w.shape[1] if n_out is None else n_out
    in_specs = [pl.BlockSpec((tm, d), lambda i, j: (i, 0)),
                pl.BlockSpec((1, d), lambda i, j: (0, 0)),
                pl.BlockSpec((d, tn), lambda i, j: (0, j))]
    args = [x, g, w]
    if act == "swiglu":
        up_off = n_out // tn
        in_specs.append(pl.BlockSpec((d, tn), lambda i, j: (0, j + up_off)))
        args.append(w)
    return pl.pallas_call(
        functools.partial(_norm_matmul_kernel, act=act),
        grid=(t // tm, n_out // tn),
        in_specs=in_specs,
        out_specs=pl.BlockSpec((tm, tn), lambda i, j: (i, j)),
        out_shape=jax.ShapeDtypeStruct((t, n_out), out_dtype),
        scratch_shapes=[pltpu.VMEM((tm, d), BF16)],
        compiler_params=_params("parallel", "arbitrary"),
        name="norm_matmul_" + act,
    )(*args)


def _mm_norm_res_kernel(*refs, n_in):
    x_refs, w_refs = refs[:n_in], refs[n_in:2 * n_in]
    g_ref, h_ref, o_ref = refs[2 * n_in:]
    y = _dot(x_refs[0][...], w_refs[0][...])
    for k in range(1, n_in):
        y = y + _dot(x_refs[k][...], w_refs[k][...])
    o_ref[...] = h_ref[...] + _rms(y, g_ref[...])


def _mm_norm_res(xs, ws, g, h, *, tm):
    t, d = h.shape
    n_in = len(xs)
    in_specs = [pl.BlockSpec((tm, x.shape[1]), lambda i: (i, 0)) for x in xs]
    in_specs += [pl.BlockSpec(w.shape, lambda i: (0, 0)) for w in ws]
    in_specs += [pl.BlockSpec((1, d), lambda i: (0, 0)), pl.BlockSpec((tm, d), lambda i: (i, 0))]
    return pl.pallas_call(
        functools.partial(_mm_norm_res_kernel, n_in=n_in),
        grid=(t // tm,),
        in_specs=in_specs,
        out_specs=pl.BlockSpec((tm, d), lambda i: (i, 0)),
        out_shape=jax.ShapeDtypeStruct((t, d), F32),
        compiler_params=_params("parallel"),
        name="matmul_norm_residual",
    )(*xs, *ws, g, h)


def _dsa_kernel(q_ref, qi_ref, kv_ref, ki_ref, wi_ref, o_ref,
                key_ref, hi_ref, lo_ref, low_ref, tie_ref, kvt_ref, rq_ref, rqi_ref, w_ref, m_ref, acc_ref, kmax_ref,
                *, tq, tk, topk, idx_bits, seq):
    i = pl.program_id(1)
    q0 = i * tq
    n_t = q0 // tk + 1
    half_rows = lax.broadcasted_iota(I32, (LANES, tq), 0) < HEAD_DIM

    @pl.when(i == 0)
    def _():
        feat_is_k = lax.broadcasted_iota(I32, (LANES, LANES), 0) < HEAD_DIM

        def blk(j, kmax2):
            off = pl.multiple_of(j * LANES, LANES)
            t = kv_ref[pl.ds(off, LANES), :].astype(F32).T
            kvt_ref[:, pl.ds(off, LANES)] = jnp.where(feat_is_k, 1.0, t).astype(BF16)
            return jnp.maximum(kmax2, jnp.sum(jnp.where(feat_is_k, t * t, 0.0), axis=0, keepdims=True))

        kmax2 = lax.fori_loop(0, seq // LANES, blk, jnp.zeros((1, LANES), F32))
        kmax_ref[...] = jnp.broadcast_to(jnp.max(kmax2, axis=1, keepdims=True), (1, LANES))

    zeros_half = jnp.zeros((HEAD_DIM, tq), F32)
    q2 = jnp.zeros((1, tq), F32)
    for pair in range(N_HEADS_A // 2):
        t = q_ref[:, pair * LANES:(pair + 1) * LANES].astype(F32).T
        even = jnp.where(half_rows, t, 0.0)
        odd = jnp.concatenate([t[HEAD_DIM:], zeros_half], axis=0)
        rq_ref[:, pl.ds(2 * pair * tq, tq)] = even.astype(BF16)
        rq_ref[:, pl.ds((2 * pair + 1) * tq, tq)] = odd.astype(BF16)
        q2 = jnp.maximum(q2, jnp.sum(t * t, axis=0, keepdims=True))
    for pair in range(IDX_HEADS // 2):
        t = qi_ref[:, pair * LANES:(pair + 1) * LANES].astype(F32).T
        swapped = jnp.concatenate([t[HEAD_DIM:], t[:HEAD_DIM]], axis=0)
        rqi_ref[:, pl.ds(2 * pair * tq, tq)] = t.astype(BF16)
        rqi_ref[:, pl.ds((2 * pair + 1) * tq, tq)] = swapped.astype(BF16)
    for h in range(IDX_HEADS):
        w_ref[h] = wi_ref[:, h * LANES:(h + 1) * LANES].T[:SUBLANES]

    qpos = lax.broadcasted_iota(I32, (tk, tq), 1) + q0
    krow = lax.broadcasted_iota(I32, (tk, tq), 0)

    def score_body(t, carry):
        off = pl.multiple_of(t * tk, tk)
        dots = _dot(ki_ref[pl.ds(off, tk), :], rqi_ref[...])
        sc = jnp.zeros((tk, tq), F32)
        for h in range(IDX_HEADS):
            sc = sc + jnp.maximum(dots[:, h * tq:(h + 1) * tq], 0.0) * w_ref[h][0:1]
        sc = jnp.where(krow + off <= qpos, sc, -jnp.inf)
        b = pltpu.bitcast(sc, I32)
        key = b ^ ((b >> 31) & 0x7FFFFFFF)
        key = jnp.where(key == -1, 0, key)
        key_ref[pl.ds(off, tk), :] = key
        hi_ref[pl.ds(off, tk), :] = (key >> 16).astype(I16)
        lo_ref[pl.ds(off, tk), :] = (key ^ 0x8000).astype(I16)
        return carry

    lax.fori_loop(0, n_t, score_body, 0)

    n_acc = 4
    one, zero = jnp.ones((PACKED, tq), BF16), jnp.zeros((PACKED, tq), BF16)

    n_main = n_t // MAIN_TILES
    n_tail = n_t - n_main * MAIN_TILES

    def count(ref, pred):
        def body(width, base, t, accs):
            accs = list(accs)
            tile = ref[pl.ds(pl.multiple_of(base + t * width, tk), width), :]
            for r in range(width // PACKED):
                x = tile[r * PACKED:(r + 1) * PACKED]
                accs[r % n_acc] = accs[r % n_acc] + jnp.where(pred(x), one, zero)
            return tuple(accs)
        accs = lax.fori_loop(0, n_main, functools.partial(body, MAIN_TILES * tk, 0), (zero,) * n_acc)
        accs = lax.fori_loop(0, n_tail, functools.partial(body, tk, n_main * MAIN_TILES * tk), accs)
        total = accs[0].astype(F32)
        for a in accs[1:]:
            total = total + a.astype(F32)
        return jnp.sum(total, axis=0, keepdims=True)

    def largest16(ref, want):
        c = count(ref, lambda x: x >= jnp.int16(0))
        ok = c >= want
        v = jnp.broadcast_to(jnp.where(ok, 0, -32768), (PACKED, tq)).astype(I32)
        above = jnp.where(ok, 0.0, c)

        def body(p, carry):
            v, above = carry
            cand = v | jnp.left_shift(jnp.int32(1), 14 - p)
            c16 = cand.astype(I16)
            c = count(ref, lambda x: x >= c16)
            ok = c >= want
            return jnp.where(ok, cand, v), jnp.where(ok, above, c)

        return lax.fori_loop(0, 15, body, (v, above))

    kf = np.float32(topk)
    thr_hi, above_hi = largest16(hi_ref, kf)
    thr_hi16 = thr_hi.astype(I16)
    thr_hi_tile = jnp.broadcast_to(thr_hi16[0:1], (tk, tq))

    def low_body(t, carry):
        rows = pl.ds(pl.multiple_of(t * tk, tk), tk)
        low_ref[rows, :] = jnp.where(hi_ref[rows, :] == thr_hi_tile, lo_ref[rows, :], jnp.int16(-32768))
        return carry

    lax.fori_loop(0, n_t, low_body, 0)
    thr_lo, above_lo = largest16(low_ref, kf - above_hi)
    thr_lo16 = thr_lo.astype(I16)
    thr = (thr_hi[0:1] << 16) | ((thr_lo[0:1] + 32768) & 0xFFFF)

    need = kf - above_hi - above_lo
    none = jnp.int16(32767)
    dead = jnp.broadcast_to(jnp.where(thr == NEG_INF_KEY, 32767, 0), (tk, tq)).astype(I16)
    thr_lo_tile = jnp.broadcast_to(thr_lo16[0:1], (tk, tq))

    def tie_body(t, carry):
        off = pl.multiple_of(t * tk, tk)
        rows = pl.ds(off, tk)
        pos = (krow + off).astype(I16) | dead
        at_lo = jnp.where(lo_ref[rows, :] == thr_lo_tile, pos, none)
        tie_ref[rows, :] = jnp.where(hi_ref[rows, :] == thr_hi_tile, at_lo, none)
        return carry

    lax.fori_loop(0, n_t, tie_body, 0)

    def cut_body(p, cut):
        cand = cut | jnp.left_shift(jnp.int32(1), idx_bits - 1 - p)
        c16 = cand.astype(I16)
        return jnp.where(count(tie_ref, lambda x: x < c16) < need, cand, cut)

    cut = lax.fori_loop(0, idx_bits, cut_body, jnp.zeros((PACKED, tq), I32))
    cut_row = jnp.where(thr != NEG_INF_KEY, cut[0:1], -1)

    acc_ref[...] = jnp.zeros(acc_ref.shape, F32)
    bound2 = jnp.max(q2, axis=1, keepdims=True) * kmax_ref[:, 0:1]
    safe = jnp.max(bound2) <= LOGIT_LIMIT * LOGIT_LIMIT

    @pl.when(safe)
    def _():
        one_t, zero_t = jnp.ones((tk, tq), BF16), jnp.zeros((tk, tq), BF16)
        cut_tile = jnp.broadcast_to(cut_row, (tk, tq)).astype(I16)

        def body(t, carry):
            rows = pl.ds(pl.multiple_of(t * tk, tk), tk)
            tied = jnp.where(tie_ref[rows, :] <= cut_tile, one_t, zero_t)
            keep = jnp.where(hi_ref[rows, :] > thr_hi_tile, one_t,
                             jnp.where(low_ref[rows, :] > thr_lo_tile, one_t, tied))
            s_all = _dot(kv_ref[rows, :], rq_ref[...])
            probs = [jnp.exp(s_all[:, h * tq:(h + 1) * tq]).astype(BF16) * keep for h in range(N_HEADS_A)]
            acc_ref[...] += _dot(kvt_ref[:, rows], jnp.concatenate(probs, axis=1))
            return carry

        lax.fori_loop(0, n_t, body, 0)

    @pl.when(jnp.logical_not(safe))
    def _():
        m_ref[...] = jnp.full(m_ref.shape, MASKED, F32)

        def body(t, carry):
            off = pl.multiple_of(t * tk, tk)
            x = key_ref[pl.ds(off, tk), :]
            tied = jnp.where(krow + off <= cut_row, 0.0, MASKED)
            bias = jnp.where(x > thr, 0.0, jnp.where(x == thr, tied, MASKED))
            s_all = _dot(kv_ref[pl.ds(off, tk), :], rq_ref[...])
            probs, alphas = [], []
            for h in range(N_HEADS_A):
                s = s_all[:, h * tq:(h + 1) * tq] + bias
                m_old = m_ref[h]
                m_new = jnp.maximum(m_old, jnp.max(s, axis=0, keepdims=True))
                probs.append(jnp.exp(s - m_new).astype(BF16))
                alphas.append(jnp.exp(m_old - m_new))
                m_ref[h] = m_new
            pv = _dot(kvt_ref[:, pl.ds(off, tk)], jnp.concatenate(probs, axis=1))
            for h in range(N_HEADS_A):
                cols = pl.ds(h * tq, tq)
                acc_ref[:, cols] = alphas[h] * acc_ref[:, cols] + pv[:, h * tq:(h + 1) * tq]
            return carry

        lax.fori_loop(0, n_t, body, 0)

    for pair in range(N_HEADS_A // 2):
        outs = []
        for h in (2 * pair, 2 * pair + 1):
            a = acc_ref[:, pl.ds(h * tq, tq)]
            outs.append(a[HEAD_DIM:] / a[0:1])
        o_ref[:, pair * LANES:(pair + 1) * LANES] = jnp.concatenate(outs, axis=0).T.astype(o_ref.dtype)


def _dsa_attention(proj, wi, *, tq=2 * BLOCK_Q, tk=512):
    b, s, _ = proj.shape
    topk = min(TOPK_MAX, s // 4)
    idx_bits = int(np.log2(s))
    assert 2 ** idx_bits == s and s < 2 ** 15 and s % tk == 0 and tk % tq == 0
    qa_w = N_HEADS_A * HEAD_DIM
    qi_w = IDX_HEADS * IDX_DIM
    return pl.pallas_call(
        functools.partial(_dsa_kernel, tq=tq, tk=tk, topk=topk, idx_bits=idx_bits, seq=s),
        grid=(b, s // tq),
        in_specs=[
            pl.BlockSpec((None, tq, qa_w), lambda bi, i: (bi, i, COL_QA // qa_w)),
            pl.BlockSpec((None, tq, qi_w), lambda bi, i: (bi, i, COL_QI // qi_w)),
            pl.BlockSpec((None, s, LANES), lambda bi, i: (bi, 0, COL_KVA // LANES)),
            pl.BlockSpec((None, s, LANES), lambda bi, i: (bi, 0, COL_KI // LANES)),
            pl.BlockSpec((None, tq, IDX_HEADS * LANES), lambda bi, i: (bi, i, 0)),
        ],
        out_specs=pl.BlockSpec((None, tq, qa_w), lambda bi, i: (bi, i, 0)),
        out_shape=jax.ShapeDtypeStruct((b, s, qa_w), BF16),
        scratch_shapes=[
            pltpu.VMEM((s, tq), I32),
            pltpu.VMEM((s, tq), I16),
            pltpu.VMEM((s, tq), I16),
            pltpu.VMEM((s, tq), I16),
            pltpu.VMEM((s, tq), I16),
            pltpu.VMEM((LANES, s), BF16),
            pltpu.VMEM((LANES, N_HEADS_A * tq), BF16),
            pltpu.VMEM((LANES, IDX_HEADS * tq), BF16),
            pltpu.VMEM((IDX_HEADS, SUBLANES, tq), F32),
            pltpu.VMEM((N_HEADS_A, 1, tq), F32),
            pltpu.VMEM((LANES, N_HEADS_A * tq), F32),
            pltpu.VMEM((1, LANES), F32),
        ],
        compiler_params=_params("parallel", "arbitrary"),
        name="dsa_attention",
    )(proj, proj, proj, proj, wi)


def _sb_kernel(q_ref, k_ref, v_ref, u_ref, o_ref, c_ref, acc_ref, *, tq, tk):
    i = pl.program_id(1)
    q0 = i * tq
    n_t = (q0 + tq - 1) // tk + 1
    n_pairs = N_HEADS_B // 2
    lane = lax.broadcasted_iota(I32, (tq, LANES), 1)
    low_half = lane < HEAD_DIM
    zero = jnp.zeros((tq, LANES), BF16)
    qh = []
    for p in range(n_pairs):
        qp = q_ref[:, p * LANES:(p + 1) * LANES]
        qh += [jnp.where(low_half, qp, zero), jnp.where(low_half, zero, qp)]
    row = lax.broadcasted_iota(I32, (tq, tk), 0) + q0
    col0 = lax.broadcasted_iota(I32, (tq, tk), 1)
    upper = u_ref[...]

    c_ref[...] = jnp.zeros(c_ref.shape, F32)
    acc_ref[...] = jnp.zeros(acc_ref.shape, F32)

    def tile(t, masked):
        off = pl.multiple_of(t * tk, tk)
        strict = (col0 + off) < row
        ks = [k_ref[pl.ds(off, tk), p * LANES:(p + 1) * LANES] for p in range(n_pairs)]
        vs = [v_ref[pl.ds(off, tk), p * LANES:(p + 1) * LANES] for p in range(n_pairs)]
        zs = [_dot_nt(qh[h], ks[h // 2]) for h in range(N_HEADS_B)]
        log_hits, fails, sums = [], [], []
        for z in zs:
            neg_abs = pltpu.bitcast(pltpu.bitcast(z, jnp.uint32) | jnp.uint32(0x80000000), F32)
            soft = jnp.log2(1.0 + jnp.exp2(neg_abs))
            log_hit = jnp.minimum(z, 0.0) - soft
            log_fail = log_hit - z
            if masked:
                log_fail = jnp.where(strict, log_fail, 0.0)
            log_hits.append(log_hit)
            fails.append(log_fail.astype(BF16))
            sums.append(jnp.sum(log_fail, axis=1, keepdims=True))
        afters = [_dot(f, upper) for f in fails]
        probs = []
        for h in range(N_HEADS_B):
            a = jnp.exp2(log_hits[h] + (afters[h] + c_ref[h]))
            if masked:
                a = jnp.where(strict, a, 0.0)
            probs.append(a.astype(BF16))
        pvs = [_dot(probs[h], vs[h // 2]) for h in range(N_HEADS_B)]
        for p in range(n_pairs):
            acc_ref[p] += jnp.where(low_half, pvs[2 * p], pvs[2 * p + 1])
        for h in range(N_HEADS_B):
            c_ref[h] += sums[h]

    tile(n_t - 1, True)

    def alive():
        return jnp.max(c_ref[...]) > DEAD_LOG2

    def cond(state):
        j, live = state
        return jnp.logical_and(j < n_t - 1, live)

    def body(state):
        j, _ = state
        tile(n_t - 2 - j, False)
        return j + 1, alive()

    lax.while_loop(cond, body, (jnp.int32(0), alive()))
    for p in range(n_pairs):
        o_ref[:, p * LANES:(p + 1) * LANES] = acc_ref[p].astype(o_ref.dtype)


def _sb_attention(proj, *, tq=BLOCK_Q, tk=256):
    b, s, _ = proj.shape
    assert s % tk == 0 and s % tq == 0
    w = N_HEADS_B * HEAD_DIM
    upper = (np.arange(tk)[:, None] > np.arange(tk)[None, :]).astype(np.float32)
    resident = dict(pipeline_mode=pl.Buffered(1))
    return pl.pallas_call(
        functools.partial(_sb_kernel, tq=tq, tk=tk),
        grid=(b, s // tq),
        in_specs=[
            pl.BlockSpec((None, tq, w), lambda bi, i: (bi, i, COL_QB // w)),
            pl.BlockSpec((None, s, w), lambda bi, i: (bi, 0, COL_KB // w), **resident),
            pl.BlockSpec((None, s, w), lambda bi, i: (bi, 0, COL_VB // w), **resident),
            pl.BlockSpec((tk, tk), lambda bi, i: (0, 0)),
        ],
        out_specs=pl.BlockSpec((None, tq, w), lambda bi, i: (bi, i, 0)),
        out_shape=jax.ShapeDtypeStruct((b, s, w), BF16),
        scratch_shapes=[
            pltpu.VMEM((N_HEADS_B, tq, 1), F32),
            pltpu.VMEM((N_HEADS_B // 2, tq, LANES), F32),
        ],
        compiler_params=_params("parallel", "arbitrary"),
        name="stick_breaking_attention",
    )(proj, proj, proj, jnp.asarray(upper, BF16))


def _gmlp_tail_kernel(u_ref, v_ref, gv_ref, bv_ref, ws_ref, bs_ref, wo_ref, g_ref, h_ref, o_ref,
                      wt_ref, vn_ref, gated_ref, *, tm):
    @pl.when(pl.program_id(0) == 0)
    def _():
        r = lax.broadcasted_iota(I32, (CHUNK, CHUNK), 0)
        c = lax.broadcasted_iota(I32, (CHUNK, CHUNK), 1)
        for g in range(C_GROUPS):
            wt_ref[g] = jnp.where(c <= r, ws_ref[g], 0.0).astype(BF16)

    v = v_ref[...].astype(F32)
    mu = jnp.mean(v, axis=-1, keepdims=True)
    var = jnp.mean(jnp.square(v - mu), axis=-1, keepdims=True)
    vn_ref[...] = (((v - mu) * lax.rsqrt(var + EPS)) * gv_ref[...] + bv_ref[...]).astype(BF16)
    cw = C_WIDTH // C_GROUPS
    for ch in range(tm // CHUNK):
        rows = slice(ch * CHUNK, (ch + 1) * CHUNK)
        for g in range(C_GROUPS):
            cols = slice(g * cw, (g + 1) * cw)
            sv = _dot(wt_ref[g], vn_ref[rows, cols]) + bs_ref[g]
            gated_ref[rows, cols] = (u_ref[rows, cols].astype(F32) * sv).astype(BF16)
    y = _dot(gated_ref[...], wo_ref[...])
    o_ref[...] = h_ref[...] + _rms(y, g_ref[...])


def _gmlp_tail(z, gv, bv, ws, bs_b, wo, g, h, *, tm=256):
    t, d = h.shape
    cw = C_WIDTH // C_GROUPS
    return pl.pallas_call(
        functools.partial(_gmlp_tail_kernel, tm=tm),
        grid=(t // tm,),
        in_specs=[
            pl.BlockSpec((tm, C_WIDTH), lambda i: (i, 0)),
            pl.BlockSpec((tm, C_WIDTH), lambda i: (i, 1)),
            pl.BlockSpec((1, C_WIDTH), lambda i: (0, 0)),
            pl.BlockSpec((1, C_WIDTH), lambda i: (0, 0)),
            pl.BlockSpec((C_GROUPS, CHUNK, CHUNK), lambda i: (0, 0, 0)),
            pl.BlockSpec((C_GROUPS, CHUNK, cw), lambda i: (0, 0, 0)),
            pl.BlockSpec((C_WIDTH, d), lambda i: (0, 0)),
            pl.BlockSpec((1, d), lambda i: (0, 0)),
            pl.BlockSpec((tm, d), lambda i: (i, 0)),
        ],
        out_specs=pl.BlockSpec((tm, d), lambda i: (i, 0)),
        out_shape=jax.ShapeDtypeStruct((t, d), F32),
        scratch_shapes=[
            pltpu.VMEM((C_GROUPS, CHUNK, CHUNK), BF16),
            pltpu.VMEM((tm, C_WIDTH), BF16),
            pltpu.VMEM((tm, C_WIDTH), BF16),
        ],
        compiler_params=_params("arbitrary"),
        name="gmlp_tail",
    )(z, z, gv, bv, ws, bs_b, wo, g, h)


def _prep_even_weights(w):
    split_points = np.cumsum(EVEN_WIDTHS)[:-1].tolist()
    qa, ka, va, qi, ki, wi, qb, kb, vb = jnp.split(w, split_points, axis=1)
    d = w.shape[0]
    main = jnp.concatenate(
        [qa * (HEAD_DIM ** -0.5), qi * (IDX_DIM ** -0.5), ka, va, ki, jnp.zeros((d, LANES - IDX_DIM), w.dtype),
         qb * (HEAD_DIM ** -0.5 * LOG2_E), kb, vb], axis=1).astype(BF16)
    wi_cols = jnp.repeat(wi * (IDX_HEADS ** -0.5), LANES, axis=1).astype(BF16)
    return main, wi_cols


def kernel(x, norm_gains, w_in_even, w_out_even, w_in_odd, spatial_w, spatial_b, v_norm_gain, v_norm_bias,
           w_out_odd, w_ffn_in, w_ffn_out):
    b, s, d = x.shape
    t = b * s
    depth = norm_gains.shape[0]
    h = x.reshape(t, d)
    tm = 512
    mix_a = N_HEADS_A * HEAD_DIM
    for layer in range(depth):
        g = norm_gains[layer][:, None, :]
        i = layer // 2
        if layer % 2 == 0:
            w_main, w_wi = _prep_even_weights(w_in_even[i])
            proj = _norm_matmul(h, g[0], w_main, act="none", tm=tm, tn=EVEN_COLS // 2, out_dtype=BF16)
            wi = _norm_matmul(h, g[0], w_wi, act="none", tm=tm, tn=IDX_HEADS * LANES, out_dtype=F32)
            proj = proj.reshape(b, s, EVEN_COLS)
            ya = _dsa_attention(proj, wi.reshape(b, s, IDX_HEADS * LANES))
            yb = _sb_attention(proj)
            wo = w_out_even[i].astype(BF16)
            h = _mm_norm_res([ya.reshape(t, mix_a), yb.reshape(t, -1)], [wo[:mix_a], wo[mix_a:]], g[1], h, tm=tm)
        else:
            z = _norm_matmul(h, g[0], w_in_odd[i].astype(BF16), act="gelu", tm=tm, tn=1024, out_dtype=BF16)
            bs_b = jnp.broadcast_to(spatial_b[i][:, :, None], (C_GROUPS, CHUNK, C_WIDTH // C_GROUPS))
            h = _gmlp_tail(z, v_norm_gain[i][None, :], v_norm_bias[i][None, :], spatial_w[i], bs_b,
                           w_out_odd[i].astype(BF16), g[1], h)
        act = _norm_matmul(h, g[2], w_ffn_in[layer].astype(BF16), act="swiglu", tm=tm, tn=D_FF // 2,
                           out_dtype=BF16, n_out=D_FF)
        h = _mm_norm_res([act], [w_ffn_out[layer].astype(BF16)], g[3], h, tm=tm)
    return h.reshape(b, s, d)
```

```python
import functools

import numpy as np
import jax
import jax.numpy as jnp
from jax import lax
from jax.experimental import pallas as pl
from jax.experimental.pallas import tpu as pltpu

F32 = jnp.float32
BF16 = jnp.bfloat16
I32 = jnp.int32
I16 = jnp.int16

D_MODEL = 1024
HEAD_DIM = 64
N_HEADS_A = 8
N_HEADS_B = 8
IDX_HEADS = 4
IDX_DIM = 64
TOPK_MAX = 256
BLOCK_Q = 128
CHUNK = 128
C_WIDTH = 2048
C_GROUPS = 16
D_FF = 2816
EPS = 1e-6
EVEN_WIDTHS = (N_HEADS_A * HEAD_DIM, HEAD_DIM, HEAD_DIM, IDX_HEADS * IDX_DIM, IDX_DIM, IDX_HEADS,
               N_HEADS_B * HEAD_DIM, N_HEADS_B * HEAD_DIM, N_HEADS_B * HEAD_DIM)

LANES = 128
SUBLANES = 8
PACKED = 16
MAIN_TILES = 4
LOGIT_LIMIT = 40.0
VMEM_LIMIT = 48 * 1024 * 1024

COL_QA = 0
COL_QI = 512
COL_KVA = 768
COL_KI = 896
COL_QB = 1024
COL_KB = 1536
COL_VB = 2048
EVEN_COLS = 2560

LOG2_E = float(np.log2(np.e))
DEAD_LOG2 = -160.0
MASKED = -1e30
NEG_INF_KEY = -2139095041


def _params(*sem):
    return pltpu.CompilerParams(dimension_semantics=sem, vmem_limit_bytes=VMEM_LIMIT)


def _dot(a, b):
    return jnp.dot(a, b, preferred_element_type=F32)


def _dot_nt(a, b):
    return lax.dot_general(a, b, (((1,), (1,)), ((), ())), preferred_element_type=F32)


def _rms(y, g):
    ms = jnp.mean(y * y, axis=-1, keepdims=True)
    return (y * lax.rsqrt(ms + EPS)) * g


def _gelu_tanh(x):
    c = np.float32(np.sqrt(2.0 / np.pi))
    return x * (0.5 * (1.0 + jnp.tanh(c * (x + 0.044715 * (x * x * x)))))


def _norm_matmul_kernel(x_ref, g_ref, w_ref, *refs, act, tn, n_side):
    o_ref = refs[n_side]
    xn = _rms(x_ref[...], g_ref[...]).astype(BF16)
    n_out = o_ref.shape[1]
    for j in range(n_out // tn):
        y = _dot(xn, w_ref[:, j * tn:(j + 1) * tn])
        if act == "gelu":
            y = _gelu_tanh(y)
        elif act == "swiglu":
            up = _dot(xn, w_ref[:, n_out + j * tn:n_out + (j + 1) * tn])
            y = (y / (1.0 + jnp.exp(-y))) * up
        o_ref[:, j * tn:(j + 1) * tn] = y.astype(o_ref.dtype)
    if n_side:
        refs[n_side + 1][...] = _dot(xn, refs[0][...])


def _norm_matmul(x, g, w, *, act, tm, tn, side_w=None):
    t, d = x.shape
    n_out = w.shape[1] // 2 if act == "swiglu" else w.shape[1]
    n_side = 0 if side_w is None else 1
    resident = dict(pipeline_mode=pl.Buffered(1))
    in_specs = [pl.BlockSpec((tm, d), lambda i: (i, 0)),
                pl.BlockSpec((1, d), lambda i: (0, 0)),
                pl.BlockSpec(w.shape, lambda i: (0, 0), **resident)]
    out_specs = [pl.BlockSpec((tm, n_out), lambda i: (i, 0))]
    out_shape = [jax.ShapeDtypeStruct((t, n_out), BF16)]
    args = [x, g, w]
    if n_side:
        in_specs.append(pl.BlockSpec(side_w.shape, lambda i: (0, 0), **resident))
        out_specs.append(pl.BlockSpec((tm, side_w.shape[1]), lambda i: (i, 0)))
        out_shape.append(jax.ShapeDtypeStruct((t, side_w.shape[1]), F32))
        args.append(side_w)
    outs = pl.pallas_call(
        functools.partial(_norm_matmul_kernel, act=act, tn=tn, n_side=n_side),
        grid=(t // tm,),
        in_specs=in_specs,
        out_specs=out_specs,
        out_shape=out_shape,
        compiler_params=_params("parallel"),
        name="norm_matmul_" + act,
    )(*args)
    return outs if n_side else outs[0]


def _mm_norm_res_kernel(*refs, n_in):
    x_refs, w_refs = refs[:n_in], refs[n_in:2 * n_in]
    g_ref, h_ref, o_ref = refs[2 * n_in:]
    y = _dot(x_refs[0][...], w_refs[0][...])
    for k in range(1, n_in):
        y = y + _dot(x_refs[k][...], w_refs[k][...])
    o_ref[...] = h_ref[...] + _rms(y, g_ref[...])


def _mm_norm_res(xs, ws, g, h, *, tm):
    t, d = h.shape
    n_in = len(xs)
    in_specs = [pl.BlockSpec((tm, x.shape[1]), lambda i: (i, 0)) for x in xs]
    in_specs += [pl.BlockSpec(w.shape, lambda i: (0, 0)) for w in ws]
    in_specs += [pl.BlockSpec((1, d), lambda i: (0, 0)), pl.BlockSpec((tm, d), lambda i: (i, 0))]
    return pl.pallas_call(
        functools.partial(_mm_norm_res_kernel, n_in=n_in),
        grid=(t // tm,),
        in_specs=in_specs,
        out_specs=pl.BlockSpec((tm, d), lambda i: (i, 0)),
        out_shape=jax.ShapeDtypeStruct((t, d), F32),
        compiler_params=_params("parallel"),
        name="matmul_norm_residual",
    )(*xs, *ws, g, h)


def _dsa_kernel(q_ref, qi_ref, kv_ref, ki_ref, wi_ref, o_ref,
                key_ref, hi_ref, lo_ref, low_ref, tie_ref, kvt_ref, rq_ref, rqi_ref, w_ref, m_ref, acc_ref, kmax_ref,
                *, tq, tk, topk, idx_bits, seq):
    i = pl.program_id(1)
    q0 = i * tq
    n_t = q0 // tk + 1
    half_rows = lax.broadcasted_iota(I32, (LANES, tq), 0) < HEAD_DIM

    @pl.when(i == 0)
    def _():
        feat_is_k = lax.broadcasted_iota(I32, (LANES, LANES), 0) < HEAD_DIM

        def blk(j, kmax2):
            off = pl.multiple_of(j * LANES, LANES)
            t = kv_ref[pl.ds(off, LANES), :].astype(F32).T
            kvt_ref[:, pl.ds(off, LANES)] = jnp.where(feat_is_k, 1.0, t).astype(BF16)
            return jnp.maximum(kmax2, jnp.sum(jnp.where(feat_is_k, t * t, 0.0), axis=0, keepdims=True))

        kmax2 = lax.fori_loop(0, seq // LANES, blk, jnp.zeros((1, LANES), F32))
        kmax_ref[...] = jnp.broadcast_to(jnp.max(kmax2, axis=1, keepdims=True), (1, LANES))

    zeros_half = jnp.zeros((HEAD_DIM, tq), F32)
    q2 = jnp.zeros((1, tq), F32)
    for pair in range(N_HEADS_A // 2):
        t = q_ref[:, pair * LANES:(pair + 1) * LANES].astype(F32).T
        even = jnp.where(half_rows, t, 0.0)
        odd = jnp.concatenate([t[HEAD_DIM:], zeros_half], axis=0)
        rq_ref[:, pl.ds(2 * pair * tq, tq)] = even.astype(BF16)
        rq_ref[:, pl.ds((2 * pair + 1) * tq, tq)] = odd.astype(BF16)
        q2 = jnp.maximum(q2, jnp.sum(t * t, axis=0, keepdims=True))
    for pair in range(IDX_HEADS // 2):
        t = qi_ref[:, pair * LANES:(pair + 1) * LANES].astype(F32).T
        swapped = jnp.concatenate([t[HEAD_DIM:], t[:HEAD_DIM]], axis=0)
        rqi_ref[:, pl.ds(2 * pair * tq, tq)] = t.astype(BF16)
        rqi_ref[:, pl.ds((2 * pair + 1) * tq, tq)] = swapped.astype(BF16)
    w_ref[...] = wi_ref[...].T[:SUBLANES]

    qpos = lax.broadcasted_iota(I32, (tk, tq), 1) + q0
    krow = lax.broadcasted_iota(I32, (tk, tq), 0)

    def score_body(t, carry):
        off = pl.multiple_of(t * tk, tk)
        dots = _dot(ki_ref[pl.ds(off, tk), :], rqi_ref[...])
        sc = jnp.zeros((tk, tq), F32)
        for h in range(IDX_HEADS):
            sc = sc + jnp.maximum(dots[:, h * tq:(h + 1) * tq], 0.0) * w_ref[h:h + 1, :]
        sc = jnp.where(krow + off <= qpos, sc, -jnp.inf)
        b = pltpu.bitcast(sc, I32)
        key = b ^ ((b >> 31) & 0x7FFFFFFF)
        key = jnp.where(key == -1, 0, key)
        key_ref[pl.ds(off, tk), :] = key
        hi_ref[pl.ds(off, tk), :] = (key >> 16).astype(I16)
        lo_ref[pl.ds(off, tk), :] = (key ^ 0x8000).astype(I16)
        return carry

    lax.fori_loop(0, n_t, score_body, 0)

    n_acc = 4
    one, zero = jnp.ones((PACKED, tq), BF16), jnp.zeros((PACKED, tq), BF16)

    n_main = n_t // MAIN_TILES
    n_tail = n_t - n_main * MAIN_TILES

    def count(ref, pred):
        def body(width, base, t, accs):
            accs = list(accs)
            tile = ref[pl.ds(pl.multiple_of(base + t * width, tk), width), :]
            for r in range(width // PACKED):
                x = tile[r * PACKED:(r + 1) * PACKED]
                accs[r % n_acc] = accs[r % n_acc] + jnp.where(pred(x), one, zero)
            return tuple(accs)
        accs = lax.fori_loop(0, n_main, functools.partial(body, MAIN_TILES * tk, 0), (zero,) * n_acc)
        accs = lax.fori_loop(0, n_tail, functools.partial(body, tk, n_main * MAIN_TILES * tk), accs)
        total = accs[0].astype(F32)
        for a in accs[1:]:
            total = total + a.astype(F32)
        return jnp.sum(total, axis=0, keepdims=True)

    def largest16(ref, want):
        c = count(ref, lambda x: x >= jnp.int16(0))
        ok = c >= want
        v = jnp.broadcast_to(jnp.where(ok, 0, -32768), (PACKED, tq)).astype(I32)
        above = jnp.where(ok, 0.0, c)

        def body(p, carry):
            v, above = carry
            cand = v | jnp.left_shift(jnp.int32(1), 14 - p)
            c16 = cand.astype(I16)
            c = count(ref, lambda x: x >= c16)
            ok = c >= want
            return jnp.where(ok, cand, v), jnp.where(ok, above, c)

        return lax.fori_loop(0, 15, body, (v, above))

    kf = np.float32(topk)
    thr_hi, above_hi = largest16(hi_ref, kf)
    thr_hi16 = thr_hi.astype(I16)
    thr_hi_tile = jnp.broadcast_to(thr_hi16[0:1], (tk, tq))

    def low_body(t, carry):
        rows = pl.ds(pl.multiple_of(t * tk, tk), tk)
        low_ref[rows, :] = jnp.where(hi_ref[rows, :] == thr_hi_tile, lo_ref[rows, :], jnp.int16(-32768))
        return carry

    lax.fori_loop(0, n_t, low_body, 0)
    thr_lo, above_lo = largest16(low_ref, kf - above_hi)
    thr_lo16 = thr_lo.astype(I16)
    thr = (thr_hi[0:1] << 16) | ((thr_lo[0:1] + 32768) & 0xFFFF)

    need = kf - above_hi - above_lo
    none = jnp.int16(32767)
    dead = jnp.broadcast_to(jnp.where(thr == NEG_INF_KEY, 32767, 0), (tk, tq)).astype(I16)
    thr_lo_tile = jnp.broadcast_to(thr_lo16[0:1], (tk, tq))

    def tie_body(t, carry):
        off = pl.multiple_of(t * tk, tk)
        rows = pl.ds(off, tk)
        pos = (krow + off).astype(I16) | dead
        at_lo = jnp.where(lo_ref[rows, :] == thr_lo_tile, pos, none)
        tie_ref[rows, :] = jnp.where(hi_ref[rows, :] == thr_hi_tile, at_lo, none)
        return carry

    lax.fori_loop(0, n_t, tie_body, 0)

    def cut_body(p, cut):
        cand = cut | jnp.left_shift(jnp.int32(1), idx_bits - 1 - p)
        c16 = cand.astype(I16)
        return jnp.where(count(tie_ref, lambda x: x < c16) < need, cand, cut)

    cut = lax.fori_loop(0, idx_bits, cut_body, jnp.zeros((PACKED, tq), I32))
    cut_row = jnp.where(thr != NEG_INF_KEY, cut[0:1], -1)

    acc_ref[...] = jnp.zeros(acc_ref.shape, F32)
    bound2 = jnp.max(q2, axis=1, keepdims=True) * kmax_ref[:, 0:1]
    safe = jnp.max(bound2) <= LOGIT_LIMIT * LOGIT_LIMIT

    @pl.when(safe)
    def _():
        one_t, zero_t = jnp.ones((tk, tq), BF16), jnp.zeros((tk, tq), BF16)
        cut_tile = jnp.broadcast_to(cut_row, (tk, tq)).astype(I16)

        def body(t, carry):
            rows = pl.ds(pl.multiple_of(t * tk, tk), tk)
            tied = jnp.where(tie_ref[rows, :] <= cut_tile, one_t, zero_t)
            keep = jnp.where(hi_ref[rows, :] > thr_hi_tile, one_t,
                             jnp.where(low_ref[rows, :] > thr_lo_tile, one_t, tied))
            s_all = _dot(kv_ref[rows, :], rq_ref[...])
            probs = [jnp.exp(s_all[:, h * tq:(h + 1) * tq]).astype(BF16) * keep for h in range(N_HEADS_A)]
            acc_ref[...] += _dot(kvt_ref[:, rows], jnp.concatenate(probs, axis=1))
            return carry

        lax.fori_loop(0, n_t, body, 0)

    @pl.when(jnp.logical_not(safe))
    def _():
        m_ref[...] = jnp.full(m_ref.shape, MASKED, F32)

        def body(t, carry):
            off = pl.multiple_of(t * tk, tk)
            x = key_ref[pl.ds(off, tk), :]
            tied = jnp.where(krow + off <= cut_row, 0.0, MASKED)
            bias = jnp.where(x > thr, 0.0, jnp.where(x == thr, tied, MASKED))
            s_all = _dot(kv_ref[pl.ds(off, tk), :], rq_ref[...])
            probs, alphas = [], []
            for h in range(N_HEADS_A):
                s = s_all[:, h * tq:(h + 1) * tq] + bias
                m_old = m_ref[h]
                m_new = jnp.maximum(m_old, jnp.max(s, axis=0, keepdims=True))
                probs.append(jnp.exp(s - m_new).astype(BF16))
                alphas.append(jnp.exp(m_old - m_new))
                m_ref[h] = m_new
            pv = _dot(kvt_ref[:, pl.ds(off, tk)], jnp.concatenate(probs, axis=1))
            for h in range(N_HEADS_A):
                cols = pl.ds(h * tq, tq)
                acc_ref[:, cols] = alphas[h] * acc_ref[:, cols] + pv[:, h * tq:(h + 1) * tq]
            return carry

        lax.fori_loop(0, n_t, body, 0)

    for pair in range(N_HEADS_A // 2):
        outs = []
        for h in (2 * pair, 2 * pair + 1):
            a = acc_ref[:, pl.ds(h * tq, tq)]
            outs.append(a[HEAD_DIM:] / a[0:1])
        o_ref[:, pair * LANES:(pair + 1) * LANES] = jnp.concatenate(outs, axis=0).T.astype(o_ref.dtype)


def _dsa_attention(proj, wi, *, tq=2 * BLOCK_Q, tk=512):
    b, s, _ = proj.shape
    topk = min(TOPK_MAX, s // 4)
    idx_bits = int(np.log2(s))
    assert 2 ** idx_bits == s and s < 2 ** 15 and s % tk == 0 and tk % tq == 0
    qa_w = N_HEADS_A * HEAD_DIM
    qi_w = IDX_HEADS * IDX_DIM
    return pl.pallas_call(
        functools.partial(_dsa_kernel, tq=tq, tk=tk, topk=topk, idx_bits=idx_bits, seq=s),
        grid=(b, s // tq),
        in_specs=[
            pl.BlockSpec((None, tq, qa_w), lambda bi, i: (bi, i, COL_QA // qa_w)),
            pl.BlockSpec((None, tq, qi_w), lambda bi, i: (bi, i, COL_QI // qi_w)),
            pl.BlockSpec((None, s, LANES), lambda bi, i: (bi, 0, COL_KVA // LANES)),
            pl.BlockSpec((None, s, LANES), lambda bi, i: (bi, 0, COL_KI // LANES)),
            pl.BlockSpec((None, tq, LANES), lambda bi, i: (bi, i, 0)),
        ],
        out_specs=pl.BlockSpec((None, tq, qa_w), lambda bi, i: (bi, i, 0)),
        out_shape=jax.ShapeDtypeStruct((b, s, qa_w), BF16),
        scratch_shapes=[
            pltpu.VMEM((s, tq), I32),
            pltpu.VMEM((s, tq), I16),
            pltpu.VMEM((s, tq), I16),
            pltpu.VMEM((s, tq), I16),
            pltpu.VMEM((s, tq), I16),
            pltpu.VMEM((LANES, s), BF16),
            pltpu.VMEM((LANES, N_HEADS_A * tq), BF16),
            pltpu.VMEM((LANES, IDX_HEADS * tq), BF16),
            pltpu.VMEM((SUBLANES, tq), F32),
            pltpu.VMEM((N_HEADS_A, 1, tq), F32),
            pltpu.VMEM((LANES, N_HEADS_A * tq), F32),
            pltpu.VMEM((1, LANES), F32),
        ],
        compiler_params=_params("parallel", "arbitrary"),
        name="dsa_attention",
    )(proj, proj, proj, proj, wi)


def _sb_kernel(q_ref, k_ref, v_ref, u_ref, o_ref, c_ref, acc_ref, *, tq, tk):
    i = pl.program_id(1)
    q0 = i * tq
    n_t = (q0 + tq - 1) // tk + 1
    n_pairs = N_HEADS_B // 2
    lane = lax.broadcasted_iota(I32, (tq, LANES), 1)
    low_half = lane < HEAD_DIM
    zero = jnp.zeros((tq, LANES), BF16)
    qh = []
    for p in range(n_pairs):
        qp = q_ref[:, p * LANES:(p + 1) * LANES]
        qh += [jnp.where(low_half, qp, zero), jnp.where(low_half, zero, qp)]
    row = lax.broadcasted_iota(I32, (tq, tk), 0) + q0
    col0 = lax.broadcasted_iota(I32, (tq, tk), 1)
    upper = u_ref[...]

    c_ref[...] = jnp.zeros(c_ref.shape, F32)
    acc_ref[...] = jnp.zeros(acc_ref.shape, F32)

    def tile(t, masked):
        off = pl.multiple_of(t * tk, tk)
        strict = (col0 + off) < row
        ks = [k_ref[pl.ds(off, tk), p * LANES:(p + 1) * LANES] for p in range(n_pairs)]
        vs = [v_ref[pl.ds(off, tk), p * LANES:(p + 1) * LANES] for p in range(n_pairs)]
        zs = [_dot_nt(qh[h], ks[h // 2]) for h in range(N_HEADS_B)]
        log_hits, fails, sums = [], [], []
        for z in zs:
            neg_abs = pltpu.bitcast(pltpu.bitcast(z, jnp.uint32) | jnp.uint32(0x80000000), F32)
            soft = jnp.log2(1.0 + jnp.exp2(neg_abs))
            log_hit = jnp.minimum(z, 0.0) - soft
            log_fail = log_hit - z
            if masked:
                log_fail = jnp.where(strict, log_fail, 0.0)
            log_hits.append(log_hit)
            fails.append(log_fail.astype(BF16))
            sums.append(jnp.sum(log_fail, axis=1, keepdims=True))
        afters = [_dot(f, upper) for f in fails]
        probs = []
        for h in range(N_HEADS_B):
            a = jnp.exp2(log_hits[h] + (afters[h] + c_ref[h]))
            if masked:
                a = jnp.where(strict, a, 0.0)
            probs.append(a.astype(BF16))
        pvs = [_dot(probs[h], vs[h // 2]) for h in range(N_HEADS_B)]
        for p in range(n_pairs):
            acc_ref[p] += jnp.where(low_half, pvs[2 * p], pvs[2 * p + 1])
        for h in range(N_HEADS_B):
            c_ref[h] += sums[h]

    tile(n_t - 1, True)

    def alive():
        return jnp.max(c_ref[...]) > DEAD_LOG2

    def cond(state):
        j, live = state
        return jnp.logical_and(j < n_t - 1, live)

    def body(state):
        j, _ = state
        tile(n_t - 2 - j, False)
        return j + 1, alive()

    lax.while_loop(cond, body, (jnp.int32(0), alive()))
    for p in range(n_pairs):
        o_ref[:, p * LANES:(p + 1) * LANES] = acc_ref[p].astype(o_ref.dtype)


def _sb_attention(proj, *, tq=BLOCK_Q, tk=256):
    b, s, _ = proj.shape
    assert s % tk == 0 and s % tq == 0
    w = N_HEADS_B * HEAD_DIM
    upper = (np.arange(tk)[:, None] > np.arange(tk)[None, :]).astype(np.float32)
    resident = dict(pipeline_mode=pl.Buffered(1))
    return pl.pallas_call(
        functools.partial(_sb_kernel, tq=tq, tk=tk),
        grid=(b, s // tq),
        in_specs=[
            pl.BlockSpec((None, tq, w), lambda bi, i: (bi, i, COL_QB // w)),
            pl.BlockSpec((None, s, w), lambda bi, i: (bi, 0, COL_KB // w), **resident),
            pl.BlockSpec((None, s, w), lambda bi, i: (bi, 0, COL_VB // w), **resident),
            pl.BlockSpec((tk, tk), lambda bi, i: (0, 0)),
        ],
        out_specs=pl.BlockSpec((None, tq, w), lambda bi, i: (bi, i, 0)),
        out_shape=jax.ShapeDtypeStruct((b, s, w), BF16),
        scratch_shapes=[
            pltpu.VMEM((N_HEADS_B, tq, 1), F32),
            pltpu.VMEM((N_HEADS_B // 2, tq, LANES), F32),
        ],
        compiler_params=_params("parallel", "arbitrary"),
        name="stick_breaking_attention",
    )(proj, proj, proj, jnp.asarray(upper, BF16))


def _gmlp_tail_kernel(u_ref, v_ref, gv_ref, bv_ref, ws_ref, bs_ref, wo_ref, g_ref, h_ref, o_ref,
                      wt_ref, vn_ref, gated_ref, *, tm):
    @pl.when(pl.program_id(0) == 0)
    def _():
        r = lax.broadcasted_iota(I32, (CHUNK, CHUNK), 0)
        c = lax.broadcasted_iota(I32, (CHUNK, CHUNK), 1)
        for g in range(C_GROUPS):
            wt_ref[g] = jnp.where(c <= r, ws_ref[g], 0.0).astype(BF16)

    v = v_ref[...].astype(F32)
    mu = jnp.mean(v, axis=-1, keepdims=True)
    var = jnp.mean(jnp.square(v - mu), axis=-1, keepdims=True)
    vn_ref[...] = (((v - mu) * lax.rsqrt(var + EPS)) * gv_ref[...] + bv_ref[...]).astype(BF16)
    cw = C_WIDTH // C_GROUPS
    chunks = [slice(ch * CHUNK, (ch + 1) * CHUNK) for ch in range(tm // CHUNK)]
    for g in range(C_GROUPS):
        cols = slice(g * cw, (g + 1) * cw)
        sv = _dot(wt_ref[g], jnp.concatenate([vn_ref[rows, cols] for rows in chunks], axis=1))
        for ch, rows in enumerate(chunks):
            mixed = sv[:, ch * cw:(ch + 1) * cw] + bs_ref[g]
            gated_ref[rows, cols] = (u_ref[rows, cols].astype(F32) * mixed).astype(BF16)
    y = _dot(gated_ref[...], wo_ref[...])
    o_ref[...] = h_ref[...] + _rms(y, g_ref[...])


def _gmlp_tail(z, gv, bv, ws, bs_b, wo, g, h, *, tm=512):
    t, d = h.shape
    cw = C_WIDTH // C_GROUPS
    return pl.pallas_call(
        functools.partial(_gmlp_tail_kernel, tm=tm),
        grid=(t // tm,),
        in_specs=[
            pl.BlockSpec((tm, C_WIDTH), lambda i: (i, 0)),
            pl.BlockSpec((tm, C_WIDTH), lambda i: (i, 1)),
            pl.BlockSpec((1, C_WIDTH), lambda i: (0, 0)),
            pl.BlockSpec((1, C_WIDTH), lambda i: (0, 0)),
            pl.BlockSpec((C_GROUPS, CHUNK, CHUNK), lambda i: (0, 0, 0)),
            pl.BlockSpec((C_GROUPS, CHUNK, cw), lambda i: (0, 0, 0)),
            pl.BlockSpec((C_WIDTH, d), lambda i: (0, 0)),
            pl.BlockSpec((1, d), lambda i: (0, 0)),
            pl.BlockSpec((tm, d), lambda i: (i, 0)),
        ],
        out_specs=pl.BlockSpec((tm, d), lambda i: (i, 0)),
        out_shape=jax.ShapeDtypeStruct((t, d), F32),
        scratch_shapes=[
            pltpu.VMEM((C_GROUPS, CHUNK, CHUNK), BF16),
            pltpu.VMEM((tm, C_WIDTH), BF16),
            pltpu.VMEM((tm, C_WIDTH), BF16),
        ],
        compiler_params=_params("arbitrary"),
        name="gmlp_tail",
    )(z, z, gv, bv, ws, bs_b, wo, g, h)


def _prep_even_weights(w):
    split_points = np.cumsum(EVEN_WIDTHS)[:-1].tolist()
    qa, ka, va, qi, ki, wi, qb, kb, vb = jnp.split(w, split_points, axis=1)
    d = w.shape[0]
    main = jnp.concatenate(
        [qa * (HEAD_DIM ** -0.5), qi * (IDX_DIM ** -0.5), ka, va, ki, jnp.zeros((d, LANES - IDX_DIM), w.dtype),
         qb * (HEAD_DIM ** -0.5 * LOG2_E), kb, vb], axis=1).astype(BF16)
    wi_cols = jnp.concatenate(
        [wi * (IDX_HEADS ** -0.5), jnp.zeros((d, LANES - IDX_HEADS), w.dtype)], axis=1).astype(BF16)
    return main, wi_cols


def kernel(x, norm_gains, w_in_even, w_out_even, w_in_odd, spatial_w, spatial_b, v_norm_gain, v_norm_bias,
           w_out_odd, w_ffn_in, w_ffn_out):
    b, s, d = x.shape
    t = b * s
    depth = norm_gains.shape[0]
    h = x.reshape(t, d)
    tm = 512
    mix_a = N_HEADS_A * HEAD_DIM
    for layer in range(depth):
        g = norm_gains[layer][:, None, :]
        i = layer // 2
        if layer % 2 == 0:
            w_main, w_wi = _prep_even_weights(w_in_even[i])
            proj, wi = _norm_matmul(h, g[0], w_main, act="none", tm=tm, tn=EVEN_COLS // 5, side_w=w_wi)
            proj = proj.reshape(b, s, EVEN_COLS)
            ya = _dsa_attention(proj, wi.reshape(b, s, LANES))
            yb = _sb_attention(proj)
            wo = w_out_even[i].astype(BF16)
            h = _mm_norm_res([ya.reshape(t, mix_a), yb.reshape(t, -1)], [wo[:mix_a], wo[mix_a:]], g[1], h, tm=tm)
        else:
            z = _norm_matmul(h, g[0], w_in_odd[i].astype(BF16), act="gelu", tm=tm, tn=1024)
            bs_b = jnp.broadcast_to(spatial_b[i][:, :, None], (C_GROUPS, CHUNK, C_WIDTH // C_GROUPS))
            h = _gmlp_tail(z, v_norm_gain[i][None, :], v_norm_bias[i][None, :], spatial_w[i], bs_b,
                           w_out_odd[i].astype(BF16), g[1], h)
        act = _norm_matmul(h, g[2], w_ffn_in[layer].astype(BF16), act="swiglu", tm=tm, tn=D_FF // 11)
        h = _mm_norm_res([act], [w_ffn_out[layer].astype(BF16)], g[3], h, tm=tm)
    return h.reshape(b, s, d)
```

```python
import functools

import numpy as np
import jax
import jax.numpy as jnp
from jax import lax
from jax.experimental import pallas as pl
from jax.experimental.pallas import tpu as pltpu

F32 = jnp.float32
BF16 = jnp.bfloat16
I32 = jnp.int32
I16 = jnp.int16

D_MODEL = 1024
HEAD_DIM = 64
N_HEADS_A = 8
N_HEADS_B = 8
IDX_HEADS = 4
IDX_DIM = 64
TOPK_MAX = 256
BLOCK_Q = 128
CHUNK = 128
C_WIDTH = 2048
C_GROUPS = 16
D_FF = 2816
EPS = 1e-6
EVEN_WIDTHS = (N_HEADS_A * HEAD_DIM, HEAD_DIM, HEAD_DIM, IDX_HEADS * IDX_DIM, IDX_DIM, IDX_HEADS,
               N_HEADS_B * HEAD_DIM, N_HEADS_B * HEAD_DIM, N_HEADS_B * HEAD_DIM)

LANES = 128
SUBLANES = 8
PACKED = 16
MAIN_TILES = 4
LOGIT_LIMIT = 40.0
VMEM_LIMIT = 48 * 1024 * 1024

COL_QA = 0
COL_QI = 512
COL_KVA = 768
COL_KI = 896
COL_QB = 1024
COL_KB = 1536
COL_VB = 2048
EVEN_COLS = 2560

LOG2_E = float(np.log2(np.e))
DEAD_LOG2 = -160.0
MASKED = -1e30
NEG_INF_KEY = -2139095041


def _params(*sem):
    return pltpu.CompilerParams(dimension_semantics=sem, vmem_limit_bytes=VMEM_LIMIT)


def _dot(a, b):
    return jnp.dot(a, b, preferred_element_type=F32)


def _dot_nt(a, b):
    return lax.dot_general(a, b, (((1,), (1,)), ((), ())), preferred_element_type=F32)


def _rms(y, g):
    ms = jnp.mean(y * y, axis=-1, keepdims=True)
    return (y * lax.rsqrt(ms + EPS)) * g


def _gelu_tanh(x):
    c = np.float32(np.sqrt(2.0 / np.pi))
    return x * (0.5 * (1.0 + jnp.tanh(c * (x + 0.044715 * (x * x * x)))))


def _norm_matmul_kernel(x_ref, g_ref, w_ref, *refs, act, tn, n_side):
    o_ref = refs[n_side]
    xn = _rms(x_ref[...], g_ref[...]).astype(BF16)
    n_out = o_ref.shape[1]
    for j in range(n_out // tn):
        y = _dot(xn, w_ref[:, j * tn:(j + 1) * tn])
        if act == "gelu":
            y = _gelu_tanh(y)
        elif act == "swiglu":
            up = _dot(xn, w_ref[:, n_out + j * tn:n_out + (j + 1) * tn])
            y = (y / (1.0 + jnp.exp(-y))) * up
        o_ref[:, j * tn:(j + 1) * tn] = y.astype(o_ref.dtype)
    if n_side:
        refs[n_side + 1][...] = _dot(xn, refs[0][...])


def _norm_matmul(x, g, w, *, act, tm, tn, side_w=None):
    t, d = x.shape
    n_out = w.shape[1] // 2 if act == "swiglu" else w.shape[1]
    n_side = 0 if side_w is None else 1
    resident = dict(pipeline_mode=pl.Buffered(1))
    in_specs = [pl.BlockSpec((tm, d), lambda i: (i, 0)),
                pl.BlockSpec((1, d), lambda i: (0, 0)),
                pl.BlockSpec(w.shape, lambda i: (0, 0), **resident)]
    out_specs = [pl.BlockSpec((tm, n_out), lambda i: (i, 0))]
    out_shape = [jax.ShapeDtypeStruct((t, n_out), BF16)]
    args = [x, g, w]
    if n_side:
        in_specs.append(pl.BlockSpec(side_w.shape, lambda i: (0, 0), **resident))
        out_specs.append(pl.BlockSpec((tm, side_w.shape[1]), lambda i: (i, 0)))
        out_shape.append(jax.ShapeDtypeStruct((t, side_w.shape[1]), F32))
        args.append(side_w)
    outs = pl.pallas_call(
        functools.partial(_norm_matmul_kernel, act=act, tn=tn, n_side=n_side),
        grid=(t // tm,),
        in_specs=in_specs,
        out_specs=out_specs,
        out_shape=out_shape,
        compiler_params=_params("parallel"),
        name="norm_matmul_" + act,
    )(*args)
    return outs if n_side else outs[0]


def _mm_norm_res_kernel(*refs, n_in):
    x_refs, w_refs = refs[:n_in], refs[n_in:2 * n_in]
    g_ref, h_ref, o_ref = refs[2 * n_in:]
    y = _dot(x_refs[0][...], w_refs[0][...])
    for k in range(1, n_in):
        y = y + _dot(x_refs[k][...], w_refs[k][...])
    o_ref[...] = h_ref[...] + _rms(y, g_ref[...])


def _mm_norm_res(xs, ws, g, h, *, tm):
    t, d = h.shape
    n_in = len(xs)
    in_specs = [pl.BlockSpec((tm, x.shape[1]), lambda i: (i, 0)) for x in xs]
    in_specs += [pl.BlockSpec(w.shape, lambda i: (0, 0)) for w in ws]
    in_specs += [pl.BlockSpec((1, d), lambda i: (0, 0)), pl.BlockSpec((tm, d), lambda i: (i, 0))]
    return pl.pallas_call(
        functools.partial(_mm_norm_res_kernel, n_in=n_in),
        grid=(t // tm,),
        in_specs=in_specs,
        out_specs=pl.BlockSpec((tm, d), lambda i: (i, 0)),
        out_shape=jax.ShapeDtypeStruct((t, d), F32),
        compiler_params=_params("parallel"),
        name="matmul_norm_residual",
    )(*xs, *ws, g, h)


def _dsa_kernel(q_ref, qi_ref, kv_ref, ki_ref, wi_ref, o_ref,
                key_ref, hi_ref, lo_ref, low_ref, tie_ref, kvt_ref, rq_ref, rqi_ref, w_ref, m_ref, acc_ref, kmax_ref, p_ref,
                *, tq, tk, topk, idx_bits, seq):
    i = pl.program_id(1)
    q0 = i * tq
    n_t = q0 // tk + 1
    half_rows = lax.broadcasted_iota(I32, (LANES, tq), 0) < HEAD_DIM

    @pl.when(i == 0)
    def _():
        feat_is_k = lax.broadcasted_iota(I32, (LANES, LANES), 0) < HEAD_DIM

        def blk(j, kmax2):
            off = pl.multiple_of(j * LANES, LANES)
            t = kv_ref[pl.ds(off, LANES), :].astype(F32).T
            kvt_ref[:, pl.ds(off, LANES)] = jnp.where(feat_is_k, 1.0, t).astype(BF16)
            return jnp.maximum(kmax2, jnp.sum(jnp.where(feat_is_k, t * t, 0.0), axis=0, keepdims=True))

        kmax2 = lax.fori_loop(0, seq // LANES, blk, jnp.zeros((1, LANES), F32))
        kmax_ref[...] = jnp.broadcast_to(jnp.max(kmax2, axis=1, keepdims=True), (1, LANES))

    zeros_half = jnp.zeros((HEAD_DIM, tq), F32)
    q2 = jnp.zeros((1, tq), F32)
    for pair in range(N_HEADS_A // 2):
        t = q_ref[:, pair * LANES:(pair + 1) * LANES].astype(F32).T
        even = jnp.where(half_rows, t, 0.0)
        odd = jnp.concatenate([t[HEAD_DIM:], zeros_half], axis=0)
        rq_ref[:, pl.ds(2 * pair * tq, tq)] = even.astype(BF16)
        rq_ref[:, pl.ds((2 * pair + 1) * tq, tq)] = odd.astype(BF16)
        q2 = jnp.maximum(q2, jnp.sum(t * t, axis=0, keepdims=True))
    for pair in range(IDX_HEADS // 2):
        t = qi_ref[:, pair * LANES:(pair + 1) * LANES].astype(F32).T
        swapped = jnp.concatenate([t[HEAD_DIM:], t[:HEAD_DIM]], axis=0)
        rqi_ref[:, pl.ds(2 * pair * tq, tq)] = t.astype(BF16)
        rqi_ref[:, pl.ds((2 * pair + 1) * tq, tq)] = swapped.astype(BF16)
    w_ref[...] = wi_ref[...].T[:SUBLANES]

    qpos = lax.broadcasted_iota(I32, (tk, tq), 1) + q0
    krow = lax.broadcasted_iota(I32, (tk, tq), 0)

    def score_body(diagonal, t, carry):
        off = pl.multiple_of(t * tk, tk)
        dots = _dot(ki_ref[pl.ds(off, tk), :], rqi_ref[...])
        sc = jnp.zeros((tk, tq), F32)
        for h in range(IDX_HEADS):
            sc = sc + jnp.maximum(dots[:, h * tq:(h + 1) * tq], 0.0) * w_ref[h:h + 1, :]
        if diagonal:
            sc = jnp.where(krow + off <= qpos, sc, -jnp.inf)
        b = pltpu.bitcast(sc, I32)
        key = b ^ ((b >> 31) & 0x7FFFFFFF)
        key = jnp.where(key == -1, 0, key)
        key_ref[pl.ds(off, tk), :] = key
        hi_ref[pl.ds(off, tk), :] = (key >> 16).astype(I16)
        lo_ref[pl.ds(off, tk), :] = (key ^ 0x8000).astype(I16)
        return carry

    lax.fori_loop(0, n_t - 1, functools.partial(score_body, False), 0)
    score_body(True, n_t - 1, 0)

    n_acc = 4
    one, zero = jnp.ones((PACKED, tq), BF16), jnp.zeros((PACKED, tq), BF16)

    n_main = n_t // MAIN_TILES
    n_tail = n_t - n_main * MAIN_TILES

    def count(ref, pred):
        def body(width, base, t, accs):
            accs = list(accs)
            tile = ref[pl.ds(pl.multiple_of(base + t * width, tk), width), :]
            for r in range(width // PACKED):
                x = tile[r * PACKED:(r + 1) * PACKED]
                accs[r % n_acc] = accs[r % n_acc] + jnp.where(pred(x), one, zero)
            return tuple(accs)
        accs = lax.fori_loop(0, n_main, functools.partial(body, MAIN_TILES * tk, 0), (zero,) * n_acc)
        accs = lax.fori_loop(0, n_tail, functools.partial(body, tk, n_main * MAIN_TILES * tk), accs)
        total = accs[0].astype(F32)
        for a in accs[1:]:
            total = total + a.astype(F32)
        return jnp.sum(total, axis=0, keepdims=True)

    def largest16(ref, want):
        c = count(ref, lambda x: x >= jnp.int16(0))
        ok = c >= want
        v = jnp.broadcast_to(jnp.where(ok, 0, -32768), (PACKED, tq)).astype(I32)
        above = jnp.where(ok, 0.0, c)

        def body(p, carry):
            v, above = carry
            cand = v | jnp.left_shift(jnp.int32(1), 14 - p)
            c16 = cand.astype(I16)
            c = count(ref, lambda x: x >= c16)
            ok = c >= want
            return jnp.where(ok, cand, v), jnp.where(ok, above, c)

        return lax.fori_loop(0, 15, body, (v, above))

    kf = np.float32(topk)
    thr_hi, above_hi = largest16(hi_ref, kf)
    thr_hi16 = thr_hi.astype(I16)
    thr_hi_tile = jnp.broadcast_to(thr_hi16[0:1], (tk, tq))

    def low_body(t, carry):
        rows = pl.ds(pl.multiple_of(t * tk, tk), tk)
        low_ref[rows, :] = jnp.where(hi_ref[rows, :] == thr_hi_tile, lo_ref[rows, :], jnp.int16(-32768))
        return carry

    lax.fori_loop(0, n_t, low_body, 0)
    thr_lo, above_lo = largest16(low_ref, kf - above_hi)
    thr_lo16 = thr_lo.astype(I16)
    thr = (thr_hi[0:1] << 16) | ((thr_lo[0:1] + 32768) & 0xFFFF)

    need = kf - above_hi - above_lo
    none = jnp.int16(32767)
    dead = jnp.broadcast_to(jnp.where(thr == NEG_INF_KEY, 32767, 0), (tk, tq)).astype(I16)
    thr_lo_tile = jnp.broadcast_to(thr_lo16[0:1], (tk, tq))

    def tie_body(t, carry):
        off = pl.multiple_of(t * tk, tk)
        rows = pl.ds(off, tk)
        pos = (krow + off).astype(I16) | dead
        at_lo = jnp.where(lo_ref[rows, :] == thr_lo_tile, pos, none)
        tie_ref[rows, :] = jnp.where(hi_ref[rows, :] == thr_hi_tile, at_lo, none)
        return carry

    lax.fori_loop(0, n_t, tie_body, 0)

    def cut_body(p, cut):
        cand = cut | jnp.left_shift(jnp.int32(1), idx_bits - 1 - p)
        c16 = cand.astype(I16)
        return jnp.where(count(tie_ref, lambda x: x < c16) < need, cand, cut)

    every_tie_taken = jnp.max(count(tie_ref, lambda x: x < none) - need) <= 0.0
    cut = lax.cond(every_tie_taken,
                   lambda: jnp.full((PACKED, tq), 2 ** idx_bits - 1, I32),
                   lambda: lax.fori_loop(0, idx_bits, cut_body, jnp.zeros((PACKED, tq), I32)))
    cut_row = jnp.where(thr != NEG_INF_KEY, cut[0:1], -1)

    acc_ref[...] = jnp.zeros(acc_ref.shape, F32)
    bound2 = jnp.max(q2, axis=1, keepdims=True) * kmax_ref[:, 0:1]
    safe = jnp.max(bound2) <= LOGIT_LIMIT * LOGIT_LIMIT

    @pl.when(safe)
    def _():
        one_t, zero_t = jnp.ones((tk, tq), BF16), jnp.zeros((tk, tq), BF16)
        cut_tile = jnp.broadcast_to(cut_row, (tk, tq)).astype(I16)

        def rows_of(t):
            return pl.ds(pl.multiple_of(t * tk, tk), tk)

        def logits(t):
            return _dot(kv_ref[rows_of(t), :], rq_ref[...])

        def store_probs(t, s_all):
            rows = rows_of(t)
            tied = jnp.where(tie_ref[rows, :] <= cut_tile, one_t, zero_t)
            keep = jnp.where(hi_ref[rows, :] > thr_hi_tile, one_t,
                             jnp.where(low_ref[rows, :] > thr_lo_tile, one_t, tied))
            for h in range(N_HEADS_A):
                cols = slice(h * tq, (h + 1) * tq)
                p_ref[:, cols] = jnp.exp(s_all[:, cols]).astype(BF16) * keep

        store_probs(0, logits(0))

        def body(t, carry):
            s_all = logits(t)
            pv = _dot(kvt_ref[:, rows_of(t - 1)], p_ref[...])
            acc_ref[...] += pv
            store_probs(t, s_all)
            return carry

        lax.fori_loop(1, n_t, body, 0)
        acc_ref[...] += _dot(kvt_ref[:, rows_of(n_t - 1)], p_ref[...])

    @pl.when(jnp.logical_not(safe))
    def _():
        m_ref[...] = jnp.full(m_ref.shape, MASKED, F32)

        def body(t, carry):
            off = pl.multiple_of(t * tk, tk)
            x = key_ref[pl.ds(off, tk), :]
            tied = jnp.where(krow + off <= cut_row, 0.0, MASKED)
            bias = jnp.where(x > thr, 0.0, jnp.where(x == thr, tied, MASKED))
            s_all = _dot(kv_ref[pl.ds(off, tk), :], rq_ref[...])
            probs, alphas = [], []
            for h in range(N_HEADS_A):
                s = s_all[:, h * tq:(h + 1) * tq] + bias
                m_old = m_ref[h]
                m_new = jnp.maximum(m_old, jnp.max(s, axis=0, keepdims=True))
                probs.append(jnp.exp(s - m_new).astype(BF16))
                alphas.append(jnp.exp(m_old - m_new))
                m_ref[h] = m_new
            pv = _dot(kvt_ref[:, pl.ds(off, tk)], jnp.concatenate(probs, axis=1))
            for h in range(N_HEADS_A):
                cols = pl.ds(h * tq, tq)
                acc_ref[:, cols] = alphas[h] * acc_ref[:, cols] + pv[:, h * tq:(h + 1) * tq]
            return carry

        lax.fori_loop(0, n_t, body, 0)

    for pair in range(N_HEADS_A // 2):
        outs = []
        for h in (2 * pair, 2 * pair + 1):
            a = acc_ref[:, pl.ds(h * tq, tq)]
            outs.append(a[HEAD_DIM:] / a[0:1])
        o_ref[:, pair * LANES:(pair + 1) * LANES] = jnp.concatenate(outs, axis=0).T.astype(o_ref.dtype)


def _dsa_attention(proj, wi, *, tq=2 * BLOCK_Q, tk=512):
    b, s, _ = proj.shape
    topk = min(TOPK_MAX, s // 4)
    idx_bits = int(np.log2(s))
    assert 2 ** idx_bits == s and s < 2 ** 15 and s % tk == 0 and tk % tq == 0
    qa_w = N_HEADS_A * HEAD_DIM
    qi_w = IDX_HEADS * IDX_DIM
    return pl.pallas_call(
        functools.partial(_dsa_kernel, tq=tq, tk=tk, topk=topk, idx_bits=idx_bits, seq=s),
        grid=(b, s // tq),
        in_specs=[
            pl.BlockSpec((None, tq, qa_w), lambda bi, i: (bi, i, COL_QA // qa_w)),
            pl.BlockSpec((None, tq, qi_w), lambda bi, i: (bi, i, COL_QI // qi_w)),
            pl.BlockSpec((None, s, LANES), lambda bi, i: (bi, 0, COL_KVA // LANES)),
            pl.BlockSpec((None, s, LANES), lambda bi, i: (bi, 0, COL_KI // LANES)),
            pl.BlockSpec((None, tq, LANES), lambda bi, i: (bi, i, 0)),
        ],
        out_specs=pl.BlockSpec((None, tq, qa_w), lambda bi, i: (bi, i, 0)),
        out_shape=jax.ShapeDtypeStruct((b, s, qa_w), BF16),
        scratch_shapes=[
            pltpu.VMEM((s, tq), I32),
            pltpu.VMEM((s, tq), I16),
            pltpu.VMEM((s, tq), I16),
            pltpu.VMEM((s, tq), I16),
            pltpu.VMEM((s, tq), I16),
            pltpu.VMEM((LANES, s), BF16),
            pltpu.VMEM((LANES, N_HEADS_A * tq), BF16),
            pltpu.VMEM((LANES, IDX_HEADS * tq), BF16),
            pltpu.VMEM((SUBLANES, tq), F32),
            pltpu.VMEM((N_HEADS_A, 1, tq), F32),
            pltpu.VMEM((LANES, N_HEADS_A * tq), F32),
            pltpu.VMEM((1, LANES), F32),
            pltpu.VMEM((tk, N_HEADS_A * tq), BF16),
        ],
        compiler_params=_params("parallel", "arbitrary"),
        name="dsa_attention",
    )(proj, proj, proj, proj, wi)


def _sb_kernel(q_ref, k_ref, v_ref, u_ref, o_ref, c_ref, acc_ref, *, tq, tk):
    i = pl.program_id(1)
    q0 = i * tq
    n_t = (q0 + tq - 1) // tk + 1
    n_pairs = N_HEADS_B // 2
    lane = lax.broadcasted_iota(I32, (tq, LANES), 1)
    low_half = lane < HEAD_DIM
    zero = jnp.zeros((tq, LANES), BF16)
    qh = []
    for p in range(n_pairs):
        qp = q_ref[:, p * LANES:(p + 1) * LANES]
        qh += [jnp.where(low_half, qp, zero), jnp.where(low_half, zero, qp)]
    row = lax.broadcasted_iota(I32, (tq, tk), 0) + q0
    col0 = lax.broadcasted_iota(I32, (tq, tk), 1)
    upper = u_ref[...]

    c_ref[...] = jnp.zeros(c_ref.shape, F32)
    acc_ref[...] = jnp.zeros(acc_ref.shape, F32)

    def tile(t, masked):
        off = pl.multiple_of(t * tk, tk)
        strict = (col0 + off) < row
        ks = [k_ref[pl.ds(off, tk), p * LANES:(p + 1) * LANES] for p in range(n_pairs)]
        vs = [v_ref[pl.ds(off, tk), p * LANES:(p + 1) * LANES] for p in range(n_pairs)]
        zs = [_dot_nt(qh[h], ks[h // 2]) for h in range(N_HEADS_B)]
        log_hits, fails, sums = [], [], []
        for z in zs:
            neg_abs = pltpu.bitcast(pltpu.bitcast(z, jnp.uint32) | jnp.uint32(0x80000000), F32)
            soft = jnp.log2(1.0 + jnp.exp2(neg_abs))
            log_hit = jnp.minimum(z, 0.0) - soft
            log_fail = log_hit - z
            if masked:
                log_fail = jnp.where(strict, log_fail, 0.0)
            log_hits.append(log_hit)
            fails.append(log_fail.astype(BF16))
            sums.append(jnp.sum(log_fail, axis=1, keepdims=True))
        afters = [_dot(f, upper) for f in fails]
        probs = []
        for h in range(N_HEADS_B):
            a = jnp.exp2(log_hits[h] + (afters[h] + c_ref[h]))
            if masked:
                a = jnp.where(strict, a, 0.0)
            probs.append(a.astype(BF16))
        pvs = [_dot(probs[h], vs[h // 2]) for h in range(N_HEADS_B)]
        for p in range(n_pairs):
            acc_ref[p] += jnp.where(low_half, pvs[2 * p], pvs[2 * p + 1])
        for h in range(N_HEADS_B):
            c_ref[h] += sums[h]

    tile(n_t - 1, True)

    def alive():
        return jnp.max(c_ref[...]) > DEAD_LOG2

    def cond(state):
        j, live = state
        return jnp.logical_and(j < n_t - 1, live)

    def body(state):
        j, _ = state
        tile(n_t - 2 - j, False)
        return j + 1, alive()

    lax.while_loop(cond, body, (jnp.int32(0), alive()))
    for p in range(n_pairs):
        o_ref[:, p * LANES:(p + 1) * LANES] = acc_ref[p].astype(o_ref.dtype)


def _sb_attention(proj, *, tq=BLOCK_Q, tk=256):
    b, s, _ = proj.shape
    assert s % tk == 0 and s % tq == 0
    w = N_HEADS_B * HEAD_DIM
    upper = (np.arange(tk)[:, None] > np.arange(tk)[None, :]).astype(np.float32)
    resident = dict(pipeline_mode=pl.Buffered(1))
    return pl.pallas_call(
        functools.partial(_sb_kernel, tq=tq, tk=tk),
        grid=(b, s // tq),
        in_specs=[
            pl.BlockSpec((None, tq, w), lambda bi, i: (bi, i, COL_QB // w)),
            pl.BlockSpec((None, s, w), lambda bi, i: (bi, 0, COL_KB // w), **resident),
            pl.BlockSpec((None, s, w), lambda bi, i: (bi, 0, COL_VB // w), **resident),
            pl.BlockSpec((tk, tk), lambda bi, i: (0, 0)),
        ],
        out_specs=pl.BlockSpec((None, tq, w), lambda bi, i: (bi, i, 0)),
        out_shape=jax.ShapeDtypeStruct((b, s, w), BF16),
        scratch_shapes=[
            pltpu.VMEM((N_HEADS_B, tq, 1), F32),
            pltpu.VMEM((N_HEADS_B // 2, tq, LANES), F32),
        ],
        compiler_params=_params("parallel", "arbitrary"),
        name="stick_breaking_attention",
    )(proj, proj, proj, jnp.asarray(upper, BF16))


def _gmlp_tail_kernel(u_ref, v_ref, gv_ref, bv_ref, ws_ref, bs_ref, wo_ref, g_ref, h_ref, o_ref,
                      wt_ref, vn_ref, gated_ref, *, tm):
    @pl.when(pl.program_id(0) == 0)
    def _():
        r = lax.broadcasted_iota(I32, (CHUNK, CHUNK), 0)
        c = lax.broadcasted_iota(I32, (CHUNK, CHUNK), 1)
        for g in range(C_GROUPS):
            wt_ref[g] = jnp.where(c <= r, ws_ref[g], 0.0).astype(BF16)

    v = v_ref[...].astype(F32)
    mu = jnp.mean(v, axis=-1, keepdims=True)
    var = jnp.mean(jnp.square(v - mu), axis=-1, keepdims=True)
    vn_ref[...] = (((v - mu) * lax.rsqrt(var + EPS)) * gv_ref[...] + bv_ref[...]).astype(BF16)
    cw = C_WIDTH // C_GROUPS
    chunks = [slice(ch * CHUNK, (ch + 1) * CHUNK) for ch in range(tm // CHUNK)]
    for g in range(C_GROUPS):
        cols = slice(g * cw, (g + 1) * cw)
        sv = _dot(wt_ref[g], jnp.concatenate([vn_ref[rows, cols] for rows in chunks], axis=1))
        for ch, rows in enumerate(chunks):
            mixed = sv[:, ch * cw:(ch + 1) * cw] + bs_ref[g]
            gated_ref[rows, cols] = (u_ref[rows, cols].astype(F32) * mixed).astype(BF16)
    y = _dot(gated_ref[...], wo_ref[...])
    o_ref[...] = h_ref[...] + _rms(y, g_ref[...])


def _gmlp_tail(z, gv, bv, ws, bs_b, wo, g, h, *, tm=512):
    t, d = h.shape
    cw = C_WIDTH // C_GROUPS
    return pl.pallas_call(
        functools.partial(_gmlp_tail_kernel, tm=tm),
        grid=(t // tm,),
        in_specs=[
            pl.BlockSpec((tm, C_WIDTH), lambda i: (i, 0)),
            pl.BlockSpec((tm, C_WIDTH), lambda i: (i, 1)),
            pl.BlockSpec((1, C_WIDTH), lambda i: (0, 0)),
            pl.BlockSpec((1, C_WIDTH), lambda i: (0, 0)),
            pl.BlockSpec((C_GROUPS, CHUNK, CHUNK), lambda i: (0, 0, 0)),
            pl.BlockSpec((C_GROUPS, CHUNK, cw), lambda i: (0, 0, 0)),
            pl.BlockSpec((C_WIDTH, d), lambda i: (0, 0)),
            pl.BlockSpec((1, d), lambda i: (0, 0)),
            pl.BlockSpec((tm, d), lambda i: (i, 0)),
        ],
        out_specs=pl.BlockSpec((tm, d), lambda i: (i, 0)),
        out_shape=jax.ShapeDtypeStruct((t, d), F32),
        scratch_shapes=[
            pltpu.VMEM((C_GROUPS, CHUNK, CHUNK), BF16),
            pltpu.VMEM((tm, C_WIDTH), BF16),
            pltpu.VMEM((tm, C_WIDTH), BF16),
        ],
        compiler_params=_params("arbitrary"),
        name="gmlp_tail",
    )(z, z, gv, bv, ws, bs_b, wo, g, h)


def _prep_even_weights(w):
    split_points = np.cumsum(EVEN_WIDTHS)[:-1].tolist()
    qa, ka, va, qi, ki, wi, qb, kb, vb = jnp.split(w, split_points, axis=1)
    d = w.shape[0]
    main = jnp.concatenate(
        [qa * (HEAD_DIM ** -0.5), qi * (IDX_DIM ** -0.5), ka, va, ki, jnp.zeros((d, LANES - IDX_DIM), w.dtype),
         qb * (HEAD_DIM ** -0.5 * LOG2_E), kb, vb], axis=1).astype(BF16)
    wi_cols = jnp.concatenate(
        [wi * (IDX_HEADS ** -0.5), jnp.zeros((d, LANES - IDX_HEADS), w.dtype)], axis=1).astype(BF16)
    return main, wi_cols


def kernel(x, norm_gains, w_in_even, w_out_even, w_in_odd, spatial_w, spatial_b, v_norm_gain, v_norm_bias,
           w_out_odd, w_ffn_in, w_ffn_out):
    b, s, d = x.shape
    t = b * s
    depth = norm_gains.shape[0]
    h = x.reshape(t, d)
    tm = 512
    mix_a = N_HEADS_A * HEAD_DIM
    for layer in range(depth):
        g = norm_gains[layer][:, None, :]
        i = layer // 2
        if layer % 2 == 0:
            w_main, w_wi = _prep_even_weights(w_in_even[i])
            proj, wi = _norm_matmul(h, g[0], w_main, act="none", tm=tm, tn=EVEN_COLS // 5, side_w=w_wi)
            proj = proj.reshape(b, s, EVEN_COLS)
            ya = _dsa_attention(proj, wi.reshape(b, s, LANES))
            yb = _sb_attention(proj)
            wo = w_out_even[i].astype(BF16)
            h = _mm_norm_res([ya.reshape(t, mix_a), yb.reshape(t, -1)], [wo[:mix_a], wo[mix_a:]], g[1], h, tm=tm)
        else:
            z = _norm_matmul(h, g[0], w_in_odd[i].astype(BF16), act="gelu", tm=tm, tn=1024)
            bs_b = jnp.broadcast_to(spatial_b[i][:, :, None], (C_GROUPS, CHUNK, C_WIDTH // C_GROUPS))
            h = _gmlp_tail(z, v_norm_gain[i][None, :], v_norm_bias[i][None, :], spatial_w[i], bs_b,
                           w_out_odd[i].astype(BF16), g[1], h)
        act = _norm_matmul(h, g[2], w_ffn_in[layer].astype(BF16), act="swiglu", tm=tm, tn=D_FF // 11)
        h = _mm_norm_res([act], [w_ffn_out[layer].astype(BF16)], g[3], h, tm=tm)
    return h.reshape(b, s, d)
```

```python
import functools

import numpy as np
import jax
import jax.numpy as jnp
from jax import lax
from jax.experimental import pallas as pl
from jax.experimental.pallas import tpu as pltpu

F32 = jnp.float32
BF16 = jnp.bfloat16
I32 = jnp.int32
I16 = jnp.int16

D_MODEL = 1024
HEAD_DIM = 64
N_HEADS_A = 8
N_HEADS_B = 8
IDX_HEADS = 4
IDX_DIM = 64
TOPK_MAX = 256
BLOCK_Q = 128
CHUNK = 128
C_WIDTH = 2048
C_GROUPS = 16
D_FF = 2816
EPS = 1e-6
EVEN_WIDTHS = (N_HEADS_A * HEAD_DIM, HEAD_DIM, HEAD_DIM, IDX_HEADS * IDX_DIM, IDX_DIM, IDX_HEADS,
               N_HEADS_B * HEAD_DIM, N_HEADS_B * HEAD_DIM, N_HEADS_B * HEAD_DIM)

LANES = 128
SUBLANES = 8
PACKED = 16
MAIN_TILES = 4
LOGIT_LIMIT = 40.0
VMEM_LIMIT = 48 * 1024 * 1024
DSA_VMEM_LIMIT = 56 * 1024 * 1024

COL_QA = 0
COL_QI = 512
COL_KVA = 768
COL_KI = 896
COL_QB = 1024
COL_KB = 1536
COL_VB = 2048
EVEN_COLS = 2560

LOG2_E = float(np.log2(np.e))
DEAD_LOG2 = -160.0
MASKED = -1e30
NEG_INF_KEY = -2139095041


def _params(*sem, vmem=VMEM_LIMIT):
    return pltpu.CompilerParams(dimension_semantics=sem, vmem_limit_bytes=vmem)


def _dot(a, b):
    return jnp.dot(a, b, preferred_element_type=F32)


def _dot_nt(a, b):
    return lax.dot_general(a, b, (((1,), (1,)), ((), ())), preferred_element_type=F32)


def _rms(y, g):
    ms = jnp.mean(y * y, axis=-1, keepdims=True)
    return (y * lax.rsqrt(ms + EPS)) * g


def _gelu_tanh(x):
    c = np.float32(np.sqrt(2.0 / np.pi))
    return x * (0.5 * (1.0 + jnp.tanh(c * (x + 0.044715 * (x * x * x)))))


def _norm_matmul_kernel(x_ref, g_ref, w_ref, *refs, act, tn, n_side):
    o_ref = refs[n_side]
    xn = _rms(x_ref[...], g_ref[...]).astype(BF16)
    n_out = o_ref.shape[1]
    for j in range(n_out // tn):
        y = _dot(xn, w_ref[:, j * tn:(j + 1) * tn])
        if act == "gelu":
            y = _gelu_tanh(y)
        elif act == "swiglu":
            up = _dot(xn, w_ref[:, n_out + j * tn:n_out + (j + 1) * tn])
            y = (y / (1.0 + jnp.exp(-y))) * up
        o_ref[:, j * tn:(j + 1) * tn] = y.astype(o_ref.dtype)
    if n_side:
        refs[n_side + 1][...] = _dot(xn, refs[0][...])


def _norm_matmul(x, g, w, *, act, tm, tn, side_w=None):
    t, d = x.shape
    n_out = w.shape[1] // 2 if act == "swiglu" else w.shape[1]
    n_side = 0 if side_w is None else 1
    resident = dict(pipeline_mode=pl.Buffered(1))
    in_specs = [pl.BlockSpec((tm, d), lambda i: (i, 0)),
                pl.BlockSpec((1, d), lambda i: (0, 0)),
                pl.BlockSpec(w.shape, lambda i: (0, 0), **resident)]
    out_specs = [pl.BlockSpec((tm, n_out), lambda i: (i, 0))]
    out_shape = [jax.ShapeDtypeStruct((t, n_out), BF16)]
    args = [x, g, w]
    if n_side:
        in_specs.append(pl.BlockSpec(side_w.shape, lambda i: (0, 0), **resident))
        out_specs.append(pl.BlockSpec((tm, side_w.shape[1]), lambda i: (i, 0)))
        out_shape.append(jax.ShapeDtypeStruct((t, side_w.shape[1]), F32))
        args.append(side_w)
    outs = pl.pallas_call(
        functools.partial(_norm_matmul_kernel, act=act, tn=tn, n_side=n_side),
        grid=(t // tm,),
        in_specs=in_specs,
        out_specs=out_specs,
        out_shape=out_shape,
        compiler_params=_params("parallel"),
        name="norm_matmul_" + act,
    )(*args)
    return outs if n_side else outs[0]


def _mm_norm_res_kernel(*refs, n_in):
    x_refs, w_refs = refs[:n_in], refs[n_in:2 * n_in]
    g_ref, h_ref, o_ref = refs[2 * n_in:]
    y = _dot(x_refs[0][...], w_refs[0][...])
    for k in range(1, n_in):
        y = y + _dot(x_refs[k][...], w_refs[k][...])
    o_ref[...] = h_ref[...] + _rms(y, g_ref[...])


def _mm_norm_res(xs, ws, g, h, *, tm):
    t, d = h.shape
    n_in = len(xs)
    in_specs = [pl.BlockSpec((tm, x.shape[1]), lambda i: (i, 0)) for x in xs]
    in_specs += [pl.BlockSpec(w.shape, lambda i: (0, 0)) for w in ws]
    in_specs += [pl.BlockSpec((1, d), lambda i: (0, 0)), pl.BlockSpec((tm, d), lambda i: (i, 0))]
    return pl.pallas_call(
        functools.partial(_mm_norm_res_kernel, n_in=n_in),
        grid=(t // tm,),
        in_specs=in_specs,
        out_specs=pl.BlockSpec((tm, d), lambda i: (i, 0)),
        out_shape=jax.ShapeDtypeStruct((t, d), F32),
        compiler_params=_params("parallel"),
        name="matmul_norm_residual",
    )(*xs, *ws, g, h)


def _dsa_kernel(q_ref, qi_ref, kv_ref, ki_ref, wi_ref, o_ref,
                hi_ref, lo_ref, tie_ref, kvt_ref, rq_ref, rqi_ref, w_ref, m_ref, acc_ref, kmax_ref,
                *, tq, tk, topk, idx_bits, seq):
    i = pl.program_id(1)
    q0 = i * tq
    n_t = q0 // tk + 1
    half_rows = lax.broadcasted_iota(I32, (LANES, tq), 0) < HEAD_DIM

    @pl.when(i == 0)
    def _():
        feat_is_k = lax.broadcasted_iota(I32, (LANES, LANES), 0) < HEAD_DIM

        def blk(j, kmax2):
            off = pl.multiple_of(j * LANES, LANES)
            t = kv_ref[pl.ds(off, LANES), :].astype(F32).T
            kvt_ref[:, pl.ds(off, LANES)] = jnp.where(feat_is_k, 1.0, t).astype(BF16)
            return jnp.maximum(kmax2, jnp.sum(jnp.where(feat_is_k, t * t, 0.0), axis=0, keepdims=True))

        kmax2 = lax.fori_loop(0, seq // LANES, blk, jnp.zeros((1, LANES), F32))
        kmax_ref[...] = jnp.broadcast_to(jnp.max(kmax2, axis=1, keepdims=True), (1, LANES))

    zeros_half = jnp.zeros((HEAD_DIM, tq), F32)
    q2 = jnp.zeros((1, tq), F32)
    for pair in range(N_HEADS_A // 2):
        t = q_ref[:, pair * LANES:(pair + 1) * LANES].astype(F32).T
        even = jnp.where(half_rows, t, 0.0)
        odd = jnp.concatenate([t[HEAD_DIM:], zeros_half], axis=0)
        rq_ref[:, pl.ds(2 * pair * tq, tq)] = even.astype(BF16)
        rq_ref[:, pl.ds((2 * pair + 1) * tq, tq)] = odd.astype(BF16)
        q2 = jnp.maximum(q2, jnp.sum(t * t, axis=0, keepdims=True))
    for pair in range(IDX_HEADS // 2):
        t = qi_ref[:, pair * LANES:(pair + 1) * LANES].astype(F32).T
        swapped = jnp.concatenate([t[HEAD_DIM:], t[:HEAD_DIM]], axis=0)
        rqi_ref[:, pl.ds(2 * pair * tq, tq)] = t.astype(BF16)
        rqi_ref[:, pl.ds((2 * pair + 1) * tq, tq)] = swapped.astype(BF16)
    w_ref[...] = wi_ref[...].T[:SUBLANES]

    qpos = lax.broadcasted_iota(I32, (tk, tq), 1) + q0
    krow = lax.broadcasted_iota(I32, (tk, tq), 0)

    def score_body(t, carry):
        off = pl.multiple_of(t * tk, tk)
        dots = _dot(ki_ref[pl.ds(off, tk), :], rqi_ref[...])
        sc = jnp.zeros((tk, tq), F32)
        for h in range(IDX_HEADS):
            sc = sc + jnp.maximum(dots[:, h * tq:(h + 1) * tq], 0.0) * w_ref[h:h + 1, :]
        sc = jnp.where(krow + off <= qpos, sc, -jnp.inf)
        b = pltpu.bitcast(sc, I32)
        key = b ^ ((b >> 31) & 0x7FFFFFFF)
        key = jnp.where(key == -1, 0, key)
        hi_ref[pl.ds(off, tk), :] = (key >> 16).astype(I16)
        lo_ref[pl.ds(off, tk), :] = (key ^ 0x8000).astype(I16)
        return carry

    lax.fori_loop(0, n_t, score_body, 0)

    n_acc = 4
    one, zero = jnp.ones((PACKED, tq), BF16), jnp.zeros((PACKED, tq), BF16)

    n_main = n_t // MAIN_TILES
    n_tail = n_t - n_main * MAIN_TILES

    def count(ref, pred):
        def body(width, base, t, accs):
            accs = list(accs)
            tile = ref[pl.ds(pl.multiple_of(base + t * width, tk), width), :]
            for r in range(width // PACKED):
                x = tile[r * PACKED:(r + 1) * PACKED]
                accs[r % n_acc] = accs[r % n_acc] + jnp.where(pred(x), one, zero)
            return tuple(accs)
        accs = lax.fori_loop(0, n_main, functools.partial(body, MAIN_TILES * tk, 0), (zero,) * n_acc)
        accs = lax.fori_loop(0, n_tail, functools.partial(body, tk, n_main * MAIN_TILES * tk), accs)
        total = accs[0].astype(F32)
        for a in accs[1:]:
            total = total + a.astype(F32)
        return jnp.sum(total, axis=0, keepdims=True)

    def largest16(ref, want):
        c = count(ref, lambda x: x >= jnp.int16(0))
        ok = c >= want
        v = jnp.broadcast_to(jnp.where(ok, 0, -32768), (PACKED, tq)).astype(I32)
        above = jnp.where(ok, 0.0, c)

        def body(p, carry):
            v, above = carry
            cand = v | jnp.left_shift(jnp.int32(1), 14 - p)
            c16 = cand.astype(I16)
            c = count(ref, lambda x: x >= c16)
            ok = c >= want
            return jnp.where(ok, cand, v), jnp.where(ok, above, c)

        return lax.fori_loop(0, 15, body, (v, above))

    kf = np.float32(topk)
    thr_hi, above_hi = largest16(hi_ref, kf)
    thr_hi16 = thr_hi.astype(I16)
    thr_hi_tile = jnp.broadcast_to(thr_hi16[0:1], (tk, tq))

    def low_body(t, carry):
        rows = pl.ds(pl.multiple_of(t * tk, tk), tk)
        lo_ref[rows, :] = jnp.where(hi_ref[rows, :] == thr_hi_tile, lo_ref[rows, :], jnp.int16(-32768))
        return carry

    lax.fori_loop(0, n_t, low_body, 0)
    thr_lo, above_lo = largest16(lo_ref, kf - above_hi)
    thr_lo16 = thr_lo.astype(I16)
    thr = (thr_hi[0:1] << 16) | ((thr_lo[0:1] + 32768) & 0xFFFF)

    need = kf - above_hi - above_lo
    none = jnp.int16(32767)
    dead = jnp.broadcast_to(jnp.where(thr == NEG_INF_KEY, 32767, 0), (tk, tq)).astype(I16)
    thr_lo_tile = jnp.broadcast_to(thr_lo16[0:1], (tk, tq))

    def tie_body(t, carry):
        off = pl.multiple_of(t * tk, tk)
        rows = pl.ds(off, tk)
        pos = (krow + off).astype(I16) | dead
        at_lo = jnp.where(lo_ref[rows, :] == thr_lo_tile, pos, none)
        tie_ref[rows, :] = jnp.where(hi_ref[rows, :] == thr_hi_tile, at_lo, none)
        return carry

    lax.fori_loop(0, n_t, tie_body, 0)

    def cut_body(p, cut):
        cand = cut | jnp.left_shift(jnp.int32(1), idx_bits - 1 - p)
        c16 = cand.astype(I16)
        return jnp.where(count(tie_ref, lambda x: x < c16) < need, cand, cut)

    cut = lax.fori_loop(0, idx_bits, cut_body, jnp.zeros((PACKED, tq), I32))
    cut_row = jnp.where(thr != NEG_INF_KEY, cut[0:1], -1)

    acc_ref[...] = jnp.zeros(acc_ref.shape, F32)
    bound2 = jnp.max(q2, axis=1, keepdims=True) * kmax_ref[:, 0:1]
    safe = jnp.max(bound2) <= LOGIT_LIMIT * LOGIT_LIMIT

    one_t, zero_t = jnp.ones((tk, tq), BF16), jnp.zeros((tk, tq), BF16)
    cut_tile = jnp.broadcast_to(cut_row, (tk, tq)).astype(I16)

    def keep_of(rows):
        tied = jnp.where(tie_ref[rows, :] <= cut_tile, one_t, zero_t)
        return jnp.where(hi_ref[rows, :] > thr_hi_tile, one_t,
                         jnp.where(lo_ref[rows, :] > thr_lo_tile, one_t, tied))

    @pl.when(safe)
    def _():
        def body(t, carry):
            rows = pl.ds(pl.multiple_of(t * tk, tk), tk)
            keep = keep_of(rows)
            s_all = _dot(kv_ref[rows, :], rq_ref[...])
            probs = [jnp.exp(s_all[:, h * tq:(h + 1) * tq]).astype(BF16) * keep for h in range(N_HEADS_A)]
            acc_ref[...] += _dot(kvt_ref[:, rows], jnp.concatenate(probs, axis=1))
            return carry

        lax.fori_loop(0, n_t, body, 0)

    @pl.when(jnp.logical_not(safe))
    def _():
        m_ref[...] = jnp.full(m_ref.shape, MASKED, F32)

        def body(t, carry):
            off = pl.multiple_of(t * tk, tk)
            bias = (keep_of(pl.ds(off, tk)).astype(F32) - 1.0) * -MASKED
            s_all = _dot(kv_ref[pl.ds(off, tk), :], rq_ref[...])
            probs, alphas = [], []
            for h in range(N_HEADS_A):
                s = s_all[:, h * tq:(h + 1) * tq] + bias
                m_old = m_ref[h]
                m_new = jnp.maximum(m_old, jnp.max(s, axis=0, keepdims=True))
                probs.append(jnp.exp(s - m_new).astype(BF16))
                alphas.append(jnp.exp(m_old - m_new))
                m_ref[h] = m_new
            pv = _dot(kvt_ref[:, pl.ds(off, tk)], jnp.concatenate(probs, axis=1))
            for h in range(N_HEADS_A):
                cols = pl.ds(h * tq, tq)
                acc_ref[:, cols] = alphas[h] * acc_ref[:, cols] + pv[:, h * tq:(h + 1) * tq]
            return carry

        lax.fori_loop(0, n_t, body, 0)

    for pair in range(N_HEADS_A // 2):
        outs = []
        for h in (2 * pair, 2 * pair + 1):
            a = acc_ref[:, pl.ds(h * tq, tq)]
            outs.append(a[HEAD_DIM:] / a[0:1])
        o_ref[:, pair * LANES:(pair + 1) * LANES] = jnp.concatenate(outs, axis=0).T.astype(o_ref.dtype)


def _dsa_attention(proj, wi, *, tq=4 * BLOCK_Q, tk=512):
    b, s, _ = proj.shape
    topk = min(TOPK_MAX, s // 4)
    idx_bits = int(np.log2(s))
    assert 2 ** idx_bits == s and s < 2 ** 15 and s % tk == 0 and tk % tq == 0
    qa_w = N_HEADS_A * HEAD_DIM
    qi_w = IDX_HEADS * IDX_DIM
    resident = dict(pipeline_mode=pl.Buffered(1))
    return pl.pallas_call(
        functools.partial(_dsa_kernel, tq=tq, tk=tk, topk=topk, idx_bits=idx_bits, seq=s),
        grid=(b, s // tq),
        in_specs=[
            pl.BlockSpec((None, tq, qa_w), lambda bi, i: (bi, i, COL_QA // qa_w)),
            pl.BlockSpec((None, tq, qi_w), lambda bi, i: (bi, i, COL_QI // qi_w)),
            pl.BlockSpec((None, s, LANES), lambda bi, i: (bi, 0, COL_KVA // LANES), **resident),
            pl.BlockSpec((None, s, LANES), lambda bi, i: (bi, 0, COL_KI // LANES), **resident),
            pl.BlockSpec((None, tq, LANES), lambda bi, i: (bi, i, 0)),
        ],
        out_specs=pl.BlockSpec((None, tq, qa_w), lambda bi, i: (bi, i, 0)),
        out_shape=jax.ShapeDtypeStruct((b, s, qa_w), BF16),
        scratch_shapes=[
            pltpu.VMEM((s, tq), I16),
            pltpu.VMEM((s, tq), I16),
            pltpu.VMEM((s, tq), I16),
            pltpu.VMEM((LANES, s), BF16),
            pltpu.VMEM((LANES, N_HEADS_A * tq), BF16),
            pltpu.VMEM((LANES, IDX_HEADS * tq), BF16),
            pltpu.VMEM((SUBLANES, tq), F32),
            pltpu.VMEM((N_HEADS_A, 1, tq), F32),
            pltpu.VMEM((LANES, N_HEADS_A * tq), F32),
            pltpu.VMEM((1, LANES), F32),
        ],
        compiler_params=_params("parallel", "arbitrary", vmem=DSA_VMEM_LIMIT),
        name="dsa_attention",
    )(proj, proj, proj, proj, wi)


def _sb_kernel(q_ref, k_ref, v_ref, u_ref, o_ref, c_ref, acc_ref, *, tq, tk):
    i = pl.program_id(1)
    q0 = i * tq
    n_t = (q0 + tq - 1) // tk + 1
    n_pairs = N_HEADS_B // 2
    lane = lax.broadcasted_iota(I32, (tq, LANES), 1)
    low_half = lane < HEAD_DIM
    zero = jnp.zeros((tq, LANES), BF16)
    qh = []
    for p in range(n_pairs):
        qp = q_ref[:, p * LANES:(p + 1) * LANES]
        qh += [jnp.where(low_half, qp, zero), jnp.where(low_half, zero, qp)]
    row = lax.broadcasted_iota(I32, (tq, tk), 0) + q0
    col0 = lax.broadcasted_iota(I32, (tq, tk), 1)
    upper = u_ref[...]

    c_ref[...] = jnp.zeros(c_ref.shape, F32)
    acc_ref[...] = jnp.zeros(acc_ref.shape, F32)

    def tile(t, masked):
        off = pl.multiple_of(t * tk, tk)
        strict = (col0 + off) < row
        ks = [k_ref[pl.ds(off, tk), p * LANES:(p + 1) * LANES] for p in range(n_pairs)]
        vs = [v_ref[pl.ds(off, tk), p * LANES:(p + 1) * LANES] for p in range(n_pairs)]
        zs = [_dot_nt(qh[h], ks[h // 2]) for h in range(N_HEADS_B)]
        log_hits, fails, sums = [], [], []
        for z in zs:
            neg_abs = pltpu.bitcast(pltpu.bitcast(z, jnp.uint32) | jnp.uint32(0x80000000), F32)
            soft = jnp.log2(1.0 + jnp.exp2(neg_abs))
            log_hit = jnp.minimum(z, 0.0) - soft
            log_fail = log_hit - z
            if masked:
                log_fail = jnp.where(strict, log_fail, 0.0)
            log_hits.append(log_hit)
            fails.append(log_fail.astype(BF16))
            sums.append(jnp.sum(log_fail, axis=1, keepdims=True))
        afters = [_dot(f, upper) for f in fails]
        probs = []
        for h in range(N_HEADS_B):
            a = jnp.exp2(log_hits[h] + (afters[h] + c_ref[h]))
            if masked:
                a = jnp.where(strict, a, 0.0)
            probs.append(a.astype(BF16))
        pvs = [_dot(probs[h], vs[h // 2]) for h in range(N_HEADS_B)]
        for p in range(n_pairs):
            acc_ref[p] += jnp.where(low_half, pvs[2 * p], pvs[2 * p + 1])
        for h in range(N_HEADS_B):
            c_ref[h] += sums[h]

    tile(n_t - 1, True)

    def alive():
        return jnp.max(c_ref[...]) > DEAD_LOG2

    def cond(state):
        j, live = state
        return jnp.logical_and(j < n_t - 1, live)

    def body(state):
        j, _ = state
        tile(n_t - 2 - j, False)
        return j + 1, alive()

    lax.while_loop(cond, body, (jnp.int32(0), alive()))
    for p in range(n_pairs):
        o_ref[:, p * LANES:(p + 1) * LANES] = acc_ref[p].astype(o_ref.dtype)


def _sb_attention(proj, *, tq=BLOCK_Q, tk=256):
    b, s, _ = proj.shape
    assert s % tk == 0 and s % tq == 0
    w = N_HEADS_B * HEAD_DIM
    upper = (np.arange(tk)[:, None] > np.arange(tk)[None, :]).astype(np.float32)
    resident = dict(pipeline_mode=pl.Buffered(1))
    return pl.pallas_call(
        functools.partial(_sb_kernel, tq=tq, tk=tk),
        grid=(b, s // tq),
        in_specs=[
            pl.BlockSpec((None, tq, w), lambda bi, i: (bi, i, COL_QB // w)),
            pl.BlockSpec((None, s, w), lambda bi, i: (bi, 0, COL_KB // w), **resident),
            pl.BlockSpec((None, s, w), lambda bi, i: (bi, 0, COL_VB // w), **resident),
            pl.BlockSpec((tk, tk), lambda bi, i: (0, 0)),
        ],
        out_specs=pl.BlockSpec((None, tq, w), lambda bi, i: (bi, i, 0)),
        out_shape=jax.ShapeDtypeStruct((b, s, w), BF16),
        scratch_shapes=[
            pltpu.VMEM((N_HEADS_B, tq, 1), F32),
            pltpu.VMEM((N_HEADS_B // 2, tq, LANES), F32),
        ],
        compiler_params=_params("parallel", "arbitrary"),
        name="stick_breaking_attention",
    )(proj, proj, proj, jnp.asarray(upper, BF16))


def _gmlp_tail_kernel(u_ref, v_ref, gv_ref, bv_ref, ws_ref, bs_ref, wo_ref, g_ref, h_ref, o_ref,
                      wt_ref, vn_ref, gated_ref, *, tm):
    @pl.when(pl.program_id(0) == 0)
    def _():
        r = lax.broadcasted_iota(I32, (CHUNK, CHUNK), 0)
        c = lax.broadcasted_iota(I32, (CHUNK, CHUNK), 1)
        for g in range(C_GROUPS):
            wt_ref[g] = jnp.where(c <= r, ws_ref[g], 0.0).astype(BF16)

    v = v_ref[...].astype(F32)
    mu = jnp.mean(v, axis=-1, keepdims=True)
    var = jnp.mean(jnp.square(v - mu), axis=-1, keepdims=True)
    vn_ref[...] = (((v - mu) * lax.rsqrt(var + EPS)) * gv_ref[...] + bv_ref[...]).astype(BF16)
    cw = C_WIDTH // C_GROUPS
    chunks = [slice(ch * CHUNK, (ch + 1) * CHUNK) for ch in range(tm // CHUNK)]
    for g in range(C_GROUPS):
        cols = slice(g * cw, (g + 1) * cw)
        sv = _dot(wt_ref[g], jnp.concatenate([vn_ref[rows, cols] for rows in chunks], axis=1))
        for ch, rows in enumerate(chunks):
            mixed = sv[:, ch * cw:(ch + 1) * cw] + bs_ref[g]
            gated_ref[rows, cols] = (u_ref[rows, cols].astype(F32) * mixed).astype(BF16)
    y = _dot(gated_ref[...], wo_ref[...])
    o_ref[...] = h_ref[...] + _rms(y, g_ref[...])


def _gmlp_tail(z, gv, bv, ws, bs_b, wo, g, h, *, tm=512):
    t, d = h.shape
    cw = C_WIDTH // C_GROUPS
    return pl.pallas_call(
        functools.partial(_gmlp_tail_kernel, tm=tm),
        grid=(t // tm,),
        in_specs=[
            pl.BlockSpec((tm, C_WIDTH), lambda i: (i, 0)),
            pl.BlockSpec((tm, C_WIDTH), lambda i: (i, 1)),
            pl.BlockSpec((1, C_WIDTH), lambda i: (0, 0)),
            pl.BlockSpec((1, C_WIDTH), lambda i: (0, 0)),
            pl.BlockSpec((C_GROUPS, CHUNK, CHUNK), lambda i: (0, 0, 0)),
            pl.BlockSpec((C_GROUPS, CHUNK, cw), lambda i: (0, 0, 0)),
            pl.BlockSpec((C_WIDTH, d), lambda i: (0, 0)),
            pl.BlockSpec((1, d), lambda i: (0, 0)),
            pl.BlockSpec((tm, d), lambda i: (i, 0)),
        ],
        out_specs=pl.BlockSpec((tm, d), lambda i: (i, 0)),
        out_shape=jax.ShapeDtypeStruct((t, d), F32),
        scratch_shapes=[
            pltpu.VMEM((C_GROUPS, CHUNK, CHUNK), BF16),
            pltpu.VMEM((tm, C_WIDTH), BF16),
            pltpu.VMEM((tm, C_WIDTH), BF16),
        ],
        compiler_params=_params("arbitrary"),
        name="gmlp_tail",
    )(z, z, gv, bv, ws, bs_b, wo, g, h)


def _prep_even_weights(w):
    split_points = np.cumsum(EVEN_WIDTHS)[:-1].tolist()
    qa, ka, va, qi, ki, wi, qb, kb, vb = jnp.split(w, split_points, axis=1)
    d = w.shape[0]
    main = jnp.concatenate(
        [qa * (HEAD_DIM ** -0.5), qi * (IDX_DIM ** -0.5), ka, va, ki, jnp.zeros((d, LANES - IDX_DIM), w.dtype),
         qb * (HEAD_DIM ** -0.5 * LOG2_E), kb, vb], axis=1).astype(BF16)
    wi_cols = jnp.concatenate(
        [wi * (IDX_HEADS ** -0.5), jnp.zeros((d, LANES - IDX_HEADS), w.dtype)], axis=1).astype(BF16)
    return main, wi_cols


def kernel(x, norm_gains, w_in_even, w_out_even, w_in_odd, spatial_w, spatial_b, v_norm_gain, v_norm_bias,
           w_out_odd, w_ffn_in, w_ffn_out):
    b, s, d = x.shape
    t = b * s
    depth = norm_gains.shape[0]
    h = x.reshape(t, d)
    tm = 512
    mix_a = N_HEADS_A * HEAD_DIM
    for layer in range(depth):
        g = norm_gains[layer][:, None, :]
        i = layer // 2
        if layer % 2 == 0:
            w_main, w_wi = _prep_even_weights(w_in_even[i])
            proj, wi = _norm_matmul(h, g[0], w_main, act="none", tm=tm, tn=EVEN_COLS // 5, side_w=w_wi)
            proj = proj.reshape(b, s, EVEN_COLS)
            ya = _dsa_attention(proj, wi.reshape(b, s, LANES))
            yb = _sb_attention(proj)
            wo = w_out_even[i].astype(BF16)
            h = _mm_norm_res([ya.reshape(t, mix_a), yb.reshape(t, -1)], [wo[:mix_a], wo[mix_a:]], g[1], h, tm=tm)
        else:
            z = _norm_matmul(h, g[0], w_in_odd[i].astype(BF16), act="gelu", tm=tm, tn=1024)
            bs_b = jnp.broadcast_to(spatial_b[i][:, :, None], (C_GROUPS, CHUNK, C_WIDTH // C_GROUPS))
            h = _gmlp_tail(z, v_norm_gain[i][None, :], v_norm_bias[i][None, :], spatial_w[i], bs_b,
                           w_out_odd[i].astype(BF16), g[1], h)
        act = _norm_matmul(h, g[2], w_ffn_in[layer].astype(BF16), act="swiglu", tm=tm, tn=D_FF // 11)
        h = _mm_norm_res([act], [w_ffn_out[layer].astype(BF16)], g[3], h, tm=tm)
    return h.reshape(b, s, d)
```

```python
import functools

import numpy as np
import jax
import jax.numpy as jnp
from jax import lax
from jax.experimental import pallas as pl
from jax.experimental.pallas import tpu as pltpu

F32 = jnp.float32
BF16 = jnp.bfloat16
I32 = jnp.int32
I16 = jnp.int16

D_MODEL = 1024
HEAD_DIM = 64
N_HEADS_A = 8
N_HEADS_B = 8
IDX_HEADS = 4
IDX_DIM = 64
TOPK_MAX = 256
BLOCK_Q = 128
CHUNK = 128
C_WIDTH = 2048
C_GROUPS = 16
D_FF = 2816
EPS = 1e-6
EVEN_WIDTHS = (N_HEADS_A * HEAD_DIM, HEAD_DIM, HEAD_DIM, IDX_HEADS * IDX_DIM, IDX_DIM, IDX_HEADS,
               N_HEADS_B * HEAD_DIM, N_HEADS_B * HEAD_DIM, N_HEADS_B * HEAD_DIM)

LANES = 128
SUBLANES = 8
PACKED = 16
MAIN_TILES = 4
LOGIT_LIMIT = 40.0
VMEM_LIMIT = 48 * 1024 * 1024
DSA_VMEM_LIMIT = 56 * 1024 * 1024

COL_QA = 0
COL_QI = 512
COL_KVA = 768
COL_KI = 896
COL_QB = 1024
COL_KB = 1536
COL_VB = 2048
EVEN_COLS = 2560

LOG2_E = float(np.log2(np.e))
DEAD_LOG2 = -160.0
MASKED = -1e30
NEG_INF_KEY = -2139095041


def _params(*sem, vmem=VMEM_LIMIT):
    return pltpu.CompilerParams(dimension_semantics=sem, vmem_limit_bytes=vmem)


def _dot(a, b):
    return jnp.dot(a, b, preferred_element_type=F32)


def _dot_nt(a, b):
    return lax.dot_general(a, b, (((1,), (1,)), ((), ())), preferred_element_type=F32)


def _rms(y, g):
    ms = jnp.mean(y * y, axis=-1, keepdims=True)
    return (y * lax.rsqrt(ms + EPS)) * g


def _gelu_tanh(x):
    c = np.float32(np.sqrt(2.0 / np.pi))
    return x * (0.5 * (1.0 + jnp.tanh(c * (x + 0.044715 * (x * x * x)))))


def _norm_matmul_kernel(x_ref, g_ref, w_ref, *refs, act, tn, n_side):
    o_ref = refs[n_side]
    xn = _rms(x_ref[...], g_ref[...]).astype(BF16)
    n_out = o_ref.shape[1]
    for j in range(n_out // tn):
        y = _dot(xn, w_ref[:, j * tn:(j + 1) * tn])
        if act == "gelu":
            y = _gelu_tanh(y)
        elif act == "swiglu":
            up = _dot(xn, w_ref[:, n_out + j * tn:n_out + (j + 1) * tn])
            y = (y / (1.0 + jnp.exp(-y))) * up
        o_ref[:, j * tn:(j + 1) * tn] = y.astype(o_ref.dtype)
    if n_side:
        refs[n_side + 1][...] = _dot(xn, refs[0][...])


def _norm_matmul(x, g, w, *, act, tm, tn, side_w=None):
    t, d = x.shape
    n_out = w.shape[1] // 2 if act == "swiglu" else w.shape[1]
    n_side = 0 if side_w is None else 1
    resident = dict(pipeline_mode=pl.Buffered(1))
    in_specs = [pl.BlockSpec((tm, d), lambda i: (i, 0)),
                pl.BlockSpec((1, d), lambda i: (0, 0)),
                pl.BlockSpec(w.shape, lambda i: (0, 0), **resident)]
    out_specs = [pl.BlockSpec((tm, n_out), lambda i: (i, 0))]
    out_shape = [jax.ShapeDtypeStruct((t, n_out), BF16)]
    args = [x, g, w]
    if n_side:
        in_specs.append(pl.BlockSpec(side_w.shape, lambda i: (0, 0), **resident))
        out_specs.append(pl.BlockSpec((tm, side_w.shape[1]), lambda i: (i, 0)))
        out_shape.append(jax.ShapeDtypeStruct((t, side_w.shape[1]), F32))
        args.append(side_w)
    outs = pl.pallas_call(
        functools.partial(_norm_matmul_kernel, act=act, tn=tn, n_side=n_side),
        grid=(t // tm,),
        in_specs=in_specs,
        out_specs=out_specs,
        out_shape=out_shape,
        compiler_params=_params("parallel"),
        name="norm_matmul_" + act,
    )(*args)
    return outs if n_side else outs[0]


def _mm_norm_res_kernel(*refs, n_in):
    x_refs, w_refs = refs[:n_in], refs[n_in:2 * n_in]
    g_ref, h_ref, o_ref = refs[2 * n_in:]
    y = _dot(x_refs[0][...], w_refs[0][...])
    for k in range(1, n_in):
        y = y + _dot(x_refs[k][...], w_refs[k][...])
    o_ref[...] = h_ref[...] + _rms(y, g_ref[...])


def _mm_norm_res(xs, ws, g, h, *, tm):
    t, d = h.shape
    n_in = len(xs)
    in_specs = [pl.BlockSpec((tm, x.shape[1]), lambda i: (i, 0)) for x in xs]
    in_specs += [pl.BlockSpec(w.shape, lambda i: (0, 0)) for w in ws]
    in_specs += [pl.BlockSpec((1, d), lambda i: (0, 0)), pl.BlockSpec((tm, d), lambda i: (i, 0))]
    return pl.pallas_call(
        functools.partial(_mm_norm_res_kernel, n_in=n_in),
        grid=(t // tm,),
        in_specs=in_specs,
        out_specs=pl.BlockSpec((tm, d), lambda i: (i, 0)),
        out_shape=jax.ShapeDtypeStruct((t, d), F32),
        compiler_params=_params("parallel"),
        name="matmul_norm_residual",
    )(*xs, *ws, g, h)


def _dsa_kernel(q_ref, qi_ref, kv_ref, ki_ref, wi_ref, o_ref,
                hi_ref, lo_ref, tie_ref, kvt_ref, rq_ref, rqi_ref, w_ref, m_ref, acc_ref, kmax_ref,
                *, tq, tk, topk, idx_bits, seq):
    i = pl.program_id(1)
    q0 = i * tq
    n_t = q0 // tk + 1
    half_rows = lax.broadcasted_iota(I32, (LANES, tq), 0) < HEAD_DIM

    @pl.when(i == 0)
    def _():
        feat_is_k = lax.broadcasted_iota(I32, (LANES, LANES), 0) < HEAD_DIM

        def blk(j, kmax2):
            off = pl.multiple_of(j * LANES, LANES)
            t = kv_ref[pl.ds(off, LANES), :].astype(F32).T
            kvt_ref[:, pl.ds(off, LANES)] = jnp.where(feat_is_k, 1.0, t).astype(BF16)
            return jnp.maximum(kmax2, jnp.sum(jnp.where(feat_is_k, t * t, 0.0), axis=0, keepdims=True))

        kmax2 = lax.fori_loop(0, seq // LANES, blk, jnp.zeros((1, LANES), F32))
        kmax_ref[...] = jnp.broadcast_to(jnp.max(kmax2, axis=1, keepdims=True), (1, LANES))

    zeros_half = jnp.zeros((HEAD_DIM, tq), F32)
    q2 = jnp.zeros((1, tq), F32)
    for pair in range(N_HEADS_A // 2):
        t = q_ref[:, pair * LANES:(pair + 1) * LANES].astype(F32).T
        even = jnp.where(half_rows, t, 0.0)
        odd = jnp.concatenate([t[HEAD_DIM:], zeros_half], axis=0)
        rq_ref[:, pl.ds(2 * pair * tq, tq)] = even.astype(BF16)
        rq_ref[:, pl.ds((2 * pair + 1) * tq, tq)] = odd.astype(BF16)
        q2 = jnp.maximum(q2, jnp.sum(t * t, axis=0, keepdims=True))
    for pair in range(IDX_HEADS // 2):
        t = qi_ref[:, pair * LANES:(pair + 1) * LANES].astype(F32).T
        swapped = jnp.concatenate([t[HEAD_DIM:], t[:HEAD_DIM]], axis=0)
        rqi_ref[:, pl.ds(2 * pair * tq, tq)] = t.astype(BF16)
        rqi_ref[:, pl.ds((2 * pair + 1) * tq, tq)] = swapped.astype(BF16)
    w_ref[...] = wi_ref[...].T[:SUBLANES]

    qpos = lax.broadcasted_iota(I32, (tk, tq), 1) + q0
    krow = lax.broadcasted_iota(I32, (tk, tq), 0)

    def score_body(t, carry):
        off = pl.multiple_of(t * tk, tk)
        dots = _dot(ki_ref[pl.ds(off, tk), :], rqi_ref[...])
        sc = jnp.zeros((tk, tq), F32)
        for h in range(IDX_HEADS):
            sc = sc + jnp.maximum(dots[:, h * tq:(h + 1) * tq], 0.0) * w_ref[h:h + 1, :]
        sc = jnp.where(krow + off <= qpos, sc, -jnp.inf)
        b = pltpu.bitcast(sc, I32)
        key = b ^ ((b >> 31) & 0x7FFFFFFF)
        key = jnp.where(key == -1, 0, key)
        hi_ref[pl.ds(off, tk), :] = (key >> 16).astype(I16)
        lo_ref[pl.ds(off, tk), :] = (key ^ 0x8000).astype(I16)
        return carry

    lax.fori_loop(0, n_t, score_body, 0)

    n_acc = 4
    one, zero = jnp.ones((PACKED, tq), BF16), jnp.zeros((PACKED, tq), BF16)

    n_main = n_t // MAIN_TILES
    n_tail = n_t - n_main * MAIN_TILES

    def count(ref, pred):
        def body(width, base, t, accs):
            accs = list(accs)
            tile = ref[pl.ds(pl.multiple_of(base + t * width, tk), width), :]
            for r in range(width // PACKED):
                x = tile[r * PACKED:(r + 1) * PACKED]
                accs[r % n_acc] = accs[r % n_acc] + jnp.where(pred(x), one, zero)
            return tuple(accs)
        accs = lax.fori_loop(0, n_main, functools.partial(body, MAIN_TILES * tk, 0), (zero,) * n_acc)
        accs = lax.fori_loop(0, n_tail, functools.partial(body, tk, n_main * MAIN_TILES * tk), accs)
        total = accs[0].astype(F32)
        for a in accs[1:]:
            total = total + a.astype(F32)
        return jnp.sum(total, axis=0, keepdims=True)

    def largest16(ref, want):
        c = count(ref, lambda x: x >= jnp.int16(0))
        ok = c >= want
        v = jnp.broadcast_to(jnp.where(ok, 0, -32768), (PACKED, tq)).astype(I32)
        above = jnp.where(ok, 0.0, c)

        def body(p, carry):
            v, above = carry
            cand = v | jnp.left_shift(jnp.int32(1), 14 - p)
            c16 = cand.astype(I16)
            c = count(ref, lambda x: x >= c16)
            ok = c >= want
            return jnp.where(ok, cand, v), jnp.where(ok, above, c)

        return lax.fori_loop(0, 15, body, (v, above))

    kf = np.float32(topk)
    thr_hi, above_hi = largest16(hi_ref, kf)
    thr_hi16 = thr_hi.astype(I16)
    thr_hi_tile = jnp.broadcast_to(thr_hi16[0:1], (tk, tq))

    def low_body(t, carry):
        rows = pl.ds(pl.multiple_of(t * tk, tk), tk)
        lo_ref[rows, :] = jnp.where(hi_ref[rows, :] == thr_hi_tile, lo_ref[rows, :], jnp.int16(-32768))
        return carry

    lax.fori_loop(0, n_t, low_body, 0)
    thr_lo, above_lo = largest16(lo_ref, kf - above_hi)
    thr_lo16 = thr_lo.astype(I16)
    thr = (thr_hi[0:1] << 16) | ((thr_lo[0:1] + 32768) & 0xFFFF)

    need = kf - above_hi - above_lo
    none = jnp.int16(32767)
    dead = jnp.broadcast_to(jnp.where(thr == NEG_INF_KEY, 32767, 0), (tk, tq)).astype(I16)
    thr_lo_tile = jnp.broadcast_to(thr_lo16[0:1], (tk, tq))

    def tie_body(t, carry):
        off = pl.multiple_of(t * tk, tk)
        rows = pl.ds(off, tk)
        pos = (krow + off).astype(I16) | dead
        at_lo = jnp.where(lo_ref[rows, :] == thr_lo_tile, pos, none)
        tie_ref[rows, :] = jnp.where(hi_ref[rows, :] == thr_hi_tile, at_lo, none)
        return carry

    lax.fori_loop(0, n_t, tie_body, 0)

    pos_bits = 32 - lax.clz(n_t * tk - 1)

    def cut_body(p, cut):
        cand = cut | jnp.left_shift(jnp.int32(1), pos_bits - 1 - p)
        c16 = cand.astype(I16)
        return jnp.where(count(tie_ref, lambda x: x < c16) < need, cand, cut)

    cut = lax.fori_loop(0, pos_bits, cut_body, jnp.zeros((PACKED, tq), I32))
    cut_row = jnp.where(thr != NEG_INF_KEY, cut[0:1], -1)

    acc_ref[...] = jnp.zeros(acc_ref.shape, F32)
    bound2 = jnp.max(q2, axis=1, keepdims=True) * kmax_ref[:, 0:1]
    safe = jnp.max(bound2) <= LOGIT_LIMIT * LOGIT_LIMIT

    one_t, zero_t = jnp.ones((tk, tq), BF16), jnp.zeros((tk, tq), BF16)
    cut_tile = jnp.broadcast_to(cut_row, (tk, tq)).astype(I16)

    def keep_of(rows):
        tied = jnp.where(tie_ref[rows, :] <= cut_tile, one_t, zero_t)
        return jnp.where(hi_ref[rows, :] > thr_hi_tile, one_t,
                         jnp.where(lo_ref[rows, :] > thr_lo_tile, one_t, tied))

    @pl.when(safe)
    def _():
        def body(t, carry):
            rows = pl.ds(pl.multiple_of(t * tk, tk), tk)
            keep = keep_of(rows)
            s_all = _dot(kv_ref[rows, :], rq_ref[...])
            probs = [jnp.exp(s_all[:, h * tq:(h + 1) * tq]).astype(BF16) * keep for h in range(N_HEADS_A)]
            acc_ref[...] += _dot(kvt_ref[:, rows], jnp.concatenate(probs, axis=1))
            return carry

        lax.fori_loop(0, n_t, body, 0)

    @pl.when(jnp.logical_not(safe))
    def _():
        m_ref[...] = jnp.full(m_ref.shape, MASKED, F32)

        def body(t, carry):
            off = pl.multiple_of(t * tk, tk)
            bias = (keep_of(pl.ds(off, tk)).astype(F32) - 1.0) * -MASKED
            s_all = _dot(kv_ref[pl.ds(off, tk), :], rq_ref[...])
            probs, alphas = [], []
            for h in range(N_HEADS_A):
                s = s_all[:, h * tq:(h + 1) * tq] + bias
                m_old = m_ref[h]
                m_new = jnp.maximum(m_old, jnp.max(s, axis=0, keepdims=True))
                probs.append(jnp.exp(s - m_new).astype(BF16))
                alphas.append(jnp.exp(m_old - m_new))
                m_ref[h] = m_new
            pv = _dot(kvt_ref[:, pl.ds(off, tk)], jnp.concatenate(probs, axis=1))
            for h in range(N_HEADS_A):
                cols = pl.ds(h * tq, tq)
                acc_ref[:, cols] = alphas[h] * acc_ref[:, cols] + pv[:, h * tq:(h + 1) * tq]
            return carry

        lax.fori_loop(0, n_t, body, 0)

    for pair in range(N_HEADS_A // 2):
        outs = []
        for h in (2 * pair, 2 * pair + 1):
            a = acc_ref[:, pl.ds(h * tq, tq)]
            outs.append(a[HEAD_DIM:] / a[0:1])
        o_ref[:, pair * LANES:(pair + 1) * LANES] = jnp.concatenate(outs, axis=0).T.astype(o_ref.dtype)


def _dsa_attention(proj, wi, *, tq=4 * BLOCK_Q, tk=512):
    b, s, _ = proj.shape
    topk = min(TOPK_MAX, s // 4)
    idx_bits = int(np.log2(s))
    assert 2 ** idx_bits == s and s < 2 ** 15 and s % tk == 0 and tk % tq == 0
    qa_w = N_HEADS_A * HEAD_DIM
    qi_w = IDX_HEADS * IDX_DIM
    resident = dict(pipeline_mode=pl.Buffered(1))
    return pl.pallas_call(
        functools.partial(_dsa_kernel, tq=tq, tk=tk, topk=topk, idx_bits=idx_bits, seq=s),
        grid=(b, s // tq),
        in_specs=[
            pl.BlockSpec((None, tq, qa_w), lambda bi, i: (bi, i, COL_QA // qa_w)),
            pl.BlockSpec((None, tq, qi_w), lambda bi, i: (bi, i, COL_QI // qi_w)),
            pl.BlockSpec((None, s, LANES), lambda bi, i: (bi, 0, COL_KVA // LANES), **resident),
            pl.BlockSpec((None, s, LANES), lambda bi, i: (bi, 0, COL_KI // LANES), **resident),
            pl.BlockSpec((None, tq, LANES), lambda bi, i: (bi, i, 0)),
        ],
        out_specs=pl.BlockSpec((None, tq, qa_w), lambda bi, i: (bi, i, 0)),
        out_shape=jax.ShapeDtypeStruct((b, s, qa_w), BF16),
        scratch_shapes=[
            pltpu.VMEM((s, tq), I16),
            pltpu.VMEM((s, tq), I16),
            pltpu.VMEM((s, tq), I16),
            pltpu.VMEM((LANES, s), BF16),
            pltpu.VMEM((LANES, N_HEADS_A * tq), BF16),
            pltpu.VMEM((LANES, IDX_HEADS * tq), BF16),
            pltpu.VMEM((SUBLANES, tq), F32),
            pltpu.VMEM((N_HEADS_A, 1, tq), F32),
            pltpu.VMEM((LANES, N_HEADS_A * tq), F32),
            pltpu.VMEM((1, LANES), F32),
        ],
        compiler_params=_params("parallel", "arbitrary", vmem=DSA_VMEM_LIMIT),
        name="dsa_attention",
    )(proj, proj, proj, proj, wi)


def _sb_kernel(q_ref, k_ref, v_ref, u_ref, o_ref, c_ref, acc_ref, *, tq, tk):
    i = pl.program_id(1)
    q0 = i * tq
    n_t = (q0 + tq - 1) // tk + 1
    n_pairs = N_HEADS_B // 2
    lane = lax.broadcasted_iota(I32, (tq, LANES), 1)
    low_half = lane < HEAD_DIM
    zero = jnp.zeros((tq, LANES), BF16)
    qh = []
    for p in range(n_pairs):
        qp = q_ref[:, p * LANES:(p + 1) * LANES]
        qh += [jnp.where(low_half, qp, zero), jnp.where(low_half, zero, qp)]
    row = lax.broadcasted_iota(I32, (tq, tk), 0) + q0
    col0 = lax.broadcasted_iota(I32, (tq, tk), 1)
    upper = u_ref[...]

    c_ref[...] = jnp.zeros(c_ref.shape, F32)
    acc_ref[...] = jnp.zeros(acc_ref.shape, F32)

    def tile(t, masked):
        off = pl.multiple_of(t * tk, tk)
        strict = (col0 + off) < row
        ks = [k_ref[pl.ds(off, tk), p * LANES:(p + 1) * LANES] for p in range(n_pairs)]
        vs = [v_ref[pl.ds(off, tk), p * LANES:(p + 1) * LANES] for p in range(n_pairs)]
        zs = [_dot_nt(qh[h], ks[h // 2]) for h in range(N_HEADS_B)]
        log_hits, fails, sums = [], [], []
        for z in zs:
            neg_abs = pltpu.bitcast(pltpu.bitcast(z, jnp.uint32) | jnp.uint32(0x80000000), F32)
            soft = jnp.log2(1.0 + jnp.exp2(neg_abs))
            log_hit = jnp.minimum(z, 0.0) - soft
            log_fail = log_hit - z
            if masked:
                log_fail = jnp.where(strict, log_fail, 0.0)
            log_hits.append(log_hit)
            fails.append(log_fail.astype(BF16))
            sums.append(jnp.sum(log_fail, axis=1, keepdims=True))
        afters = [_dot(f, upper) for f in fails]
        probs = []
        for h in range(N_HEADS_B):
            a = jnp.exp2(log_hits[h] + (afters[h] + c_ref[h]))
            if masked:
                a = jnp.where(strict, a, 0.0)
            probs.append(a.astype(BF16))
        pvs = [_dot(probs[h], vs[h // 2]) for h in range(N_HEADS_B)]
        for p in range(n_pairs):
            acc_ref[p] += jnp.where(low_half, pvs[2 * p], pvs[2 * p + 1])
        for h in range(N_HEADS_B):
            c_ref[h] += sums[h]

    tile(n_t - 1, True)

    def alive():
        return jnp.max(c_ref[...]) > DEAD_LOG2

    def cond(state):
        j, live = state
        return jnp.logical_and(j < n_t - 1, live)

    def body(state):
        j, _ = state
        tile(n_t - 2 - j, False)
        return j + 1, alive()

    lax.while_loop(cond, body, (jnp.int32(0), alive()))
    for p in range(n_pairs):
        o_ref[:, p * LANES:(p + 1) * LANES] = acc_ref[p].astype(o_ref.dtype)


def _sb_attention(proj, *, tq=2 * BLOCK_Q, tk=256):
    b, s, _ = proj.shape
    assert s % tk == 0 and s % tq == 0
    w = N_HEADS_B * HEAD_DIM
    upper = (np.arange(tk)[:, None] > np.arange(tk)[None, :]).astype(np.float32)
    resident = dict(pipeline_mode=pl.Buffered(1))
    return pl.pallas_call(
        functools.partial(_sb_kernel, tq=tq, tk=tk),
        grid=(b, s // tq),
        in_specs=[
            pl.BlockSpec((None, tq, w), lambda bi, i: (bi, i, COL_QB // w)),
            pl.BlockSpec((None, s, w), lambda bi, i: (bi, 0, COL_KB // w), **resident),
            pl.BlockSpec((None, s, w), lambda bi, i: (bi, 0, COL_VB // w), **resident),
            pl.BlockSpec((tk, tk), lambda bi, i: (0, 0)),
        ],
        out_specs=pl.BlockSpec((None, tq, w), lambda bi, i: (bi, i, 0)),
        out_shape=jax.ShapeDtypeStruct((b, s, w), BF16),
        scratch_shapes=[
            pltpu.VMEM((N_HEADS_B, tq, 1), F32),
            pltpu.VMEM((N_HEADS_B // 2, tq, LANES), F32),
        ],
        compiler_params=_params("parallel", "arbitrary"),
        name="stick_breaking_attention",
    )(proj, proj, proj, jnp.asarray(upper, BF16))


def _gmlp_tail_kernel(u_ref, v_ref, gv_ref, bv_ref, ws_ref, bs_ref, wo_ref, g_ref, h_ref, o_ref,
                      wt_ref, vn_ref, gated_ref, *, tm):
    @pl.when(pl.program_id(0) == 0)
    def _():
        r = lax.broadcasted_iota(I32, (CHUNK, CHUNK), 0)
        c = lax.broadcasted_iota(I32, (CHUNK, CHUNK), 1)
        for g in range(C_GROUPS):
            wt_ref[g] = jnp.where(c <= r, ws_ref[g], 0.0).astype(BF16)

    v = v_ref[...].astype(F32)
    mu = jnp.mean(v, axis=-1, keepdims=True)
    var = jnp.mean(jnp.square(v - mu), axis=-1, keepdims=True)
    vn_ref[...] = (((v - mu) * lax.rsqrt(var + EPS)) * gv_ref[...] + bv_ref[...]).astype(BF16)
    cw = C_WIDTH // C_GROUPS
    chunks = [slice(ch * CHUNK, (ch + 1) * CHUNK) for ch in range(tm // CHUNK)]
    for g in range(C_GROUPS):
        cols = slice(g * cw, (g + 1) * cw)
        sv = _dot(wt_ref[g], jnp.concatenate([vn_ref[rows, cols] for rows in chunks], axis=1))
        for ch, rows in enumerate(chunks):
            mixed = sv[:, ch * cw:(ch + 1) * cw] + bs_ref[g]
            gated_ref[rows, cols] = (u_ref[rows, cols].astype(F32) * mixed).astype(BF16)
    y = _dot(gated_ref[...], wo_ref[...])
    o_ref[...] = h_ref[...] + _rms(y, g_ref[...])


def _gmlp_tail(z, gv, bv, ws, bs_b, wo, g, h, *, tm=512):
    t, d = h.shape
    cw = C_WIDTH // C_GROUPS
    return pl.pallas_call(
        functools.partial(_gmlp_tail_kernel, tm=tm),
        grid=(t // tm,),
        in_specs=[
            pl.BlockSpec((tm, C_WIDTH), lambda i: (i, 0)),
            pl.BlockSpec((tm, C_WIDTH), lambda i: (i, 1)),
            pl.BlockSpec((1, C_WIDTH), lambda i: (0, 0)),
            pl.BlockSpec((1, C_WIDTH), lambda i: (0, 0)),
            pl.BlockSpec((C_GROUPS, CHUNK, CHUNK), lambda i: (0, 0, 0)),
            pl.BlockSpec((C_GROUPS, CHUNK, cw), lambda i: (0, 0, 0)),
            pl.BlockSpec((C_WIDTH, d), lambda i: (0, 0)),
            pl.BlockSpec((1, d), lambda i: (0, 0)),
            pl.BlockSpec((tm, d), lambda i: (i, 0)),
        ],
        out_specs=pl.BlockSpec((tm, d), lambda i: (i, 0)),
        out_shape=jax.ShapeDtypeStruct((t, d), F32),
        scratch_shapes=[
            pltpu.VMEM((C_GROUPS, CHUNK, CHUNK), BF16),
            pltpu.VMEM((tm, C_WIDTH), BF16),
            pltpu.VMEM((tm, C_WIDTH), BF16),
        ],
        compiler_params=_params("arbitrary"),
        name="gmlp_tail",
    )(z, z, gv, bv, ws, bs_b, wo, g, h)


def _prep_even_weights(w):
    split_points = np.cumsum(EVEN_WIDTHS)[:-1].tolist()
    qa, ka, va, qi, ki, wi, qb, kb, vb = jnp.split(w, split_points, axis=1)
    d = w.shape[0]
    main = jnp.concatenate(
        [qa * (HEAD_DIM ** -0.5), qi * (IDX_DIM ** -0.5), ka, va, ki, jnp.zeros((d, LANES - IDX_DIM), w.dtype),
         qb * (HEAD_DIM ** -0.5 * LOG2_E), kb, vb], axis=1).astype(BF16)
    wi_cols = jnp.concatenate(
        [wi * (IDX_HEADS ** -0.5), jnp.zeros((d, LANES - IDX_HEADS), w.dtype)], axis=1).astype(BF16)
    return main, wi_cols


def kernel(x, norm_gains, w_in_even, w_out_even, w_in_odd, spatial_w, spatial_b, v_norm_gain, v_norm_bias,
           w_out_odd, w_ffn_in, w_ffn_out):
    b, s, d = x.shape
    t = b * s
    depth = norm_gains.shape[0]
    h = x.reshape(t, d)
    tm = 512
    mix_a = N_HEADS_A * HEAD_DIM
    for layer in range(depth):
        g = norm_gains[layer][:, None, :]
        i = layer // 2
        if layer % 2 == 0:
            w_main, w_wi = _prep_even_weights(w_in_even[i])
            proj, wi = _norm_matmul(h, g[0], w_main, act="none", tm=tm, tn=EVEN_COLS // 5, side_w=w_wi)
            proj = proj.reshape(b, s, EVEN_COLS)
            ya = _dsa_attention(proj, wi.reshape(b, s, LANES))
            yb = _sb_attention(proj)
            wo = w_out_even[i].astype(BF16)
            h = _mm_norm_res([ya.reshape(t, mix_a), yb.reshape(t, -1)], [wo[:mix_a], wo[mix_a:]], g[1], h, tm=tm)
        else:
            z = _norm_matmul(h, g[0], w_in_odd[i].astype(BF16), act="gelu", tm=tm, tn=1024)
            bs_b = jnp.broadcast_to(spatial_b[i][:, :, None], (C_GROUPS, CHUNK, C_WIDTH // C_GROUPS))
            h = _gmlp_tail(z, v_norm_gain[i][None, :], v_norm_bias[i][None, :], spatial_w[i], bs_b,
                           w_out_odd[i].astype(BF16), g[1], h)
        act = _norm_matmul(h, g[2], w_ffn_in[layer].astype(BF16), act="swiglu", tm=tm, tn=D_FF // 11)
        h = _mm_norm_res([act], [w_ffn_out[layer].astype(BF16)], g[3], h, tm=tm)
    return h.reshape(b, s, d)
```

```python
import functools

import numpy as np
import jax
import jax.numpy as jnp
from jax import lax
from jax.experimental import pallas as pl
from jax.experimental.pallas import tpu as pltpu

F32 = jnp.float32
BF16 = jnp.bfloat16
I32 = jnp.int32
I16 = jnp.int16

D_MODEL = 1024
HEAD_DIM = 64
N_HEADS_A = 8
N_HEADS_B = 8
IDX_HEADS = 4
IDX_DIM = 64
TOPK_MAX = 256
BLOCK_Q = 128
CHUNK = 128
C_WIDTH = 2048
C_GROUPS = 16
D_FF = 2816
EPS = 1e-6
EVEN_WIDTHS = (N_HEADS_A * HEAD_DIM, HEAD_DIM, HEAD_DIM, IDX_HEADS * IDX_DIM, IDX_DIM, IDX_HEADS,
               N_HEADS_B * HEAD_DIM, N_HEADS_B * HEAD_DIM, N_HEADS_B * HEAD_DIM)

LANES = 128
SUBLANES = 8
PACKED = 16
MAIN_TILES = 4
LOGIT_LIMIT = 40.0
VMEM_LIMIT = 48 * 1024 * 1024
DSA_VMEM_LIMIT = 56 * 1024 * 1024

COL_QA = 0
COL_QI = 512
COL_KVA = 768
COL_KI = 896
COL_QB = 1024
COL_KB = 1536
COL_VB = 2048
EVEN_COLS = 2560

LOG2_E = float(np.log2(np.e))
DEAD_LOG2 = -160.0
MASKED = -1e30
NEG_INF_KEY = -2139095041


def _params(*sem, vmem=VMEM_LIMIT):
    return pltpu.CompilerParams(dimension_semantics=sem, vmem_limit_bytes=vmem)


def _dot(a, b):
    return jnp.dot(a, b, preferred_element_type=F32)


def _dot_nt(a, b):
    return lax.dot_general(a, b, (((1,), (1,)), ((), ())), preferred_element_type=F32)


def _rms(y, g):
    ms = jnp.mean(y * y, axis=-1, keepdims=True)
    return (y * lax.rsqrt(ms + EPS)) * g


def _gelu_tanh(x):
    c = np.float32(np.sqrt(2.0 / np.pi))
    return x * (0.5 * (1.0 + jnp.tanh(c * (x + 0.044715 * (x * x * x)))))


def _norm_matmul_kernel(x_ref, g_ref, w_ref, *refs, act, tn, n_side):
    o_ref = refs[n_side]
    xn = _rms(x_ref[...], g_ref[...]).astype(BF16)
    n_out = o_ref.shape[1]
    for j in range(n_out // tn):
        y = _dot(xn, w_ref[:, j * tn:(j + 1) * tn])
        if act == "gelu":
            y = _gelu_tanh(y)
        elif act == "swiglu":
            up = _dot(xn, w_ref[:, n_out + j * tn:n_out + (j + 1) * tn])
            y = (y / (1.0 + jnp.exp(-y))) * up
        o_ref[:, j * tn:(j + 1) * tn] = y.astype(o_ref.dtype)
    if n_side:
        refs[n_side + 1][...] = _dot(xn, refs[0][...])


def _norm_matmul(x, g, w, *, act, tm, tn, side_w=None):
    t, d = x.shape
    n_out = w.shape[1] // 2 if act == "swiglu" else w.shape[1]
    n_side = 0 if side_w is None else 1
    resident = dict(pipeline_mode=pl.Buffered(1))
    in_specs = [pl.BlockSpec((tm, d), lambda i: (i, 0)),
                pl.BlockSpec((1, d), lambda i: (0, 0)),
                pl.BlockSpec(w.shape, lambda i: (0, 0), **resident)]
    out_specs = [pl.BlockSpec((tm, n_out), lambda i: (i, 0))]
    out_shape = [jax.ShapeDtypeStruct((t, n_out), BF16)]
    args = [x, g, w]
    if n_side:
        in_specs.append(pl.BlockSpec(side_w.shape, lambda i: (0, 0), **resident))
        out_specs.append(pl.BlockSpec((tm, side_w.shape[1]), lambda i: (i, 0)))
        out_shape.append(jax.ShapeDtypeStruct((t, side_w.shape[1]), F32))
        args.append(side_w)
    outs = pl.pallas_call(
        functools.partial(_norm_matmul_kernel, act=act, tn=tn, n_side=n_side),
        grid=(t // tm,),
        in_specs=in_specs,
        out_specs=out_specs,
        out_shape=out_shape,
        compiler_params=_params("parallel"),
        name="norm_matmul_" + act,
    )(*args)
    return outs if n_side else outs[0]


def _mm_norm_res_kernel(*refs, n_in):
    x_refs, w_refs = refs[:n_in], refs[n_in:2 * n_in]
    g_ref, h_ref, o_ref = refs[2 * n_in:]
    y = _dot(x_refs[0][...], w_refs[0][...])
    for k in range(1, n_in):
        y = y + _dot(x_refs[k][...], w_refs[k][...])
    o_ref[...] = h_ref[...] + _rms(y, g_ref[...])


def _mm_norm_res(xs, ws, g, h, *, tm):
    t, d = h.shape
    n_in = len(xs)
    in_specs = [pl.BlockSpec((tm, x.shape[1]), lambda i: (i, 0)) for x in xs]
    in_specs += [pl.BlockSpec(w.shape, lambda i: (0, 0)) for w in ws]
    in_specs += [pl.BlockSpec((1, d), lambda i: (0, 0)), pl.BlockSpec((tm, d), lambda i: (i, 0))]
    return pl.pallas_call(
        functools.partial(_mm_norm_res_kernel, n_in=n_in),
        grid=(t // tm,),
        in_specs=in_specs,
        out_specs=pl.BlockSpec((tm, d), lambda i: (i, 0)),
        out_shape=jax.ShapeDtypeStruct((t, d), F32),
        compiler_params=_params("parallel"),
        name="matmul_norm_residual",
    )(*xs, *ws, g, h)


def _dsa_kernel(q_ref, qi_ref, kv_ref, ki_ref, wi_ref, o_ref,
                hi_ref, lo_ref, tie_ref, kvt_ref, rq_ref, rqi_ref, w_ref, m_ref, acc_ref, kmax_ref,
                *, tq, tk, topk, idx_bits, seq):
    i = pl.program_id(1)
    q0 = i * tq
    n_t = q0 // tk + 1
    half_rows = lax.broadcasted_iota(I32, (LANES, tq), 0) < HEAD_DIM

    @pl.when(i == 0)
    def _():
        feat_is_k = lax.broadcasted_iota(I32, (LANES, LANES), 0) < HEAD_DIM

        def blk(j, kmax2):
            off = pl.multiple_of(j * LANES, LANES)
            t = kv_ref[pl.ds(off, LANES), :].astype(F32).T
            kvt_ref[:, pl.ds(off, LANES)] = jnp.where(feat_is_k, 1.0, t).astype(BF16)
            return jnp.maximum(kmax2, jnp.sum(jnp.where(feat_is_k, t * t, 0.0), axis=0, keepdims=True))

        kmax2 = lax.fori_loop(0, seq // LANES, blk, jnp.zeros((1, LANES), F32))
        kmax_ref[...] = jnp.broadcast_to(jnp.max(kmax2, axis=1, keepdims=True), (1, LANES))

    zeros_half = jnp.zeros((HEAD_DIM, tq), F32)
    q2 = jnp.zeros((1, tq), F32)
    for pair in range(N_HEADS_A // 2):
        t = q_ref[:, pair * LANES:(pair + 1) * LANES].astype(F32).T
        even = jnp.where(half_rows, t, 0.0)
        odd = jnp.concatenate([t[HEAD_DIM:], zeros_half], axis=0)
        rq_ref[:, pl.ds(2 * pair * tq, tq)] = even.astype(BF16)
        rq_ref[:, pl.ds((2 * pair + 1) * tq, tq)] = odd.astype(BF16)
        q2 = jnp.maximum(q2, jnp.sum(t * t, axis=0, keepdims=True))
    for pair in range(IDX_HEADS // 2):
        t = qi_ref[:, pair * LANES:(pair + 1) * LANES].astype(F32).T
        swapped = jnp.concatenate([t[HEAD_DIM:], t[:HEAD_DIM]], axis=0)
        rqi_ref[:, pl.ds(2 * pair * tq, tq)] = t.astype(BF16)
        rqi_ref[:, pl.ds((2 * pair + 1) * tq, tq)] = swapped.astype(BF16)
    w_ref[...] = wi_ref[...].T[:SUBLANES]

    qpos = lax.broadcasted_iota(I32, (tk, tq), 1) + q0
    krow = lax.broadcasted_iota(I32, (tk, tq), 0)

    def score_body(t, carry):
        off = pl.multiple_of(t * tk, tk)
        dots = _dot(ki_ref[pl.ds(off, tk), :], rqi_ref[...])
        sc = jnp.zeros((tk, tq), F32)
        for h in range(IDX_HEADS):
            sc = sc + jnp.maximum(dots[:, h * tq:(h + 1) * tq], 0.0) * w_ref[h:h + 1, :]
        sc = jnp.where(krow + off <= qpos, sc, -jnp.inf)
        b = pltpu.bitcast(sc, I32)
        key = b ^ ((b >> 31) & 0x7FFFFFFF)
        key = jnp.where(key == -1, 0, key)
        hi_ref[pl.ds(off, tk), :] = (key >> 16).astype(I16)
        lo_ref[pl.ds(off, tk), :] = (key ^ 0x8000).astype(I16)
        return carry

    lax.fori_loop(0, n_t, score_body, 0)

    n_acc = 4
    one, zero = jnp.ones((PACKED, tq), BF16), jnp.zeros((PACKED, tq), BF16)

    n_main = n_t // MAIN_TILES
    n_tail = n_t - n_main * MAIN_TILES

    def count(ref, pred):
        def body(width, base, t, accs):
            accs = list(accs)
            tile = ref[pl.ds(pl.multiple_of(base + t * width, tk), width), :]
            for r in range(width // PACKED):
                x = tile[r * PACKED:(r + 1) * PACKED]
                accs[r % n_acc] = accs[r % n_acc] + jnp.where(pred(x), one, zero)
            return tuple(accs)
        accs = lax.fori_loop(0, n_main, functools.partial(body, MAIN_TILES * tk, 0), (zero,) * n_acc)
        accs = lax.fori_loop(0, n_tail, functools.partial(body, tk, n_main * MAIN_TILES * tk), accs)
        total = accs[0].astype(F32)
        for a in accs[1:]:
            total = total + a.astype(F32)
        return jnp.sum(total, axis=0, keepdims=True)

    def largest16(ref, want):
        c = count(ref, lambda x: x >= jnp.int16(0))
        ok = c >= want
        v = jnp.broadcast_to(jnp.where(ok, 0, -32768), (PACKED, tq)).astype(I32)
        above = jnp.where(ok, 0.0, c)

        def body(p, carry):
            v, above = carry
            cand = v | jnp.left_shift(jnp.int32(1), 14 - p)
            c16 = cand.astype(I16)
            c = count(ref, lambda x: x >= c16)
            ok = c >= want
            return jnp.where(ok, cand, v), jnp.where(ok, above, c)

        return lax.fori_loop(0, 15, body, (v, above))

    kf = np.float32(topk)
    thr_hi, above_hi = largest16(hi_ref, kf)
    thr_hi16 = thr_hi.astype(I16)
    thr_hi_tile = jnp.broadcast_to(thr_hi16[0:1], (tk, tq))

    def low_body(t, carry):
        rows = pl.ds(pl.multiple_of(t * tk, tk), tk)
        lo_ref[rows, :] = jnp.where(hi_ref[rows, :] == thr_hi_tile, lo_ref[rows, :], jnp.int16(-32768))
        return carry

    lax.fori_loop(0, n_t, low_body, 0)
    thr_lo, above_lo = largest16(lo_ref, kf - above_hi)
    thr_lo16 = thr_lo.astype(I16)
    thr = (thr_hi[0:1] << 16) | ((thr_lo[0:1] + 32768) & 0xFFFF)

    need = kf - above_hi - above_lo
    none = jnp.int16(32767)
    dead = jnp.broadcast_to(jnp.where(thr == NEG_INF_KEY, 32767, 0), (tk, tq)).astype(I16)
    thr_lo_tile = jnp.broadcast_to(thr_lo16[0:1], (tk, tq))

    def tie_body(t, carry):
        off = pl.multiple_of(t * tk, tk)
        rows = pl.ds(off, tk)
        pos = (krow + off).astype(I16) | dead
        at_lo = jnp.where(lo_ref[rows, :] == thr_lo_tile, pos, none)
        tie_ref[rows, :] = jnp.where(hi_ref[rows, :] == thr_hi_tile, at_lo, none)
        return carry

    lax.fori_loop(0, n_t, tie_body, 0)

    pos_bits = 32 - lax.clz(n_t * tk - 1)

    def cut_body(p, cut):
        cand = cut | jnp.left_shift(jnp.int32(1), pos_bits - 1 - p)
        c16 = cand.astype(I16)
        return jnp.where(count(tie_ref, lambda x: x < c16) < need, cand, cut)

    cut = lax.fori_loop(0, pos_bits, cut_body, jnp.zeros((PACKED, tq), I32))
    cut_row = jnp.where(thr != NEG_INF_KEY, cut[0:1], -1)

    acc_ref[...] = jnp.zeros(acc_ref.shape, F32)
    bound2 = jnp.max(q2, axis=1, keepdims=True) * kmax_ref[:, 0:1]
    safe = jnp.max(bound2) <= LOGIT_LIMIT * LOGIT_LIMIT

    one_t, zero_t = jnp.ones((tk, tq), BF16), jnp.zeros((tk, tq), BF16)
    cut_tile = jnp.broadcast_to(cut_row, (tk, tq)).astype(I16)

    def keep_of(rows):
        tied = jnp.where(tie_ref[rows, :] <= cut_tile, one_t, zero_t)
        return jnp.where(hi_ref[rows, :] > thr_hi_tile, one_t,
                         jnp.where(lo_ref[rows, :] > thr_lo_tile, one_t, tied))

    @pl.when(safe)
    def _():
        def body(t, carry):
            rows = pl.ds(pl.multiple_of(t * tk, tk), tk)
            keep = keep_of(rows)
            s_all = _dot(kv_ref[rows, :], rq_ref[...])
            probs = [jnp.exp(s_all[:, h * tq:(h + 1) * tq]).astype(BF16) * keep for h in range(N_HEADS_A)]
            acc_ref[...] += _dot(kvt_ref[:, rows], jnp.concatenate(probs, axis=1))
            return carry

        lax.fori_loop(0, n_t, body, 0)

    @pl.when(jnp.logical_not(safe))
    def _():
        m_ref[...] = jnp.full(m_ref.shape, MASKED, F32)

        def body(t, carry):
            off = pl.multiple_of(t * tk, tk)
            bias = (keep_of(pl.ds(off, tk)).astype(F32) - 1.0) * -MASKED
            s_all = _dot(kv_ref[pl.ds(off, tk), :], rq_ref[...])
            probs, alphas = [], []
            for h in range(N_HEADS_A):
                s = s_all[:, h * tq:(h + 1) * tq] + bias
                m_old = m_ref[h]
                m_new = jnp.maximum(m_old, jnp.max(s, axis=0, keepdims=True))
                probs.append(jnp.exp(s - m_new).astype(BF16))
                alphas.append(jnp.exp(m_old - m_new))
                m_ref[h] = m_new
            pv = _dot(kvt_ref[:, pl.ds(off, tk)], jnp.concatenate(probs, axis=1))
            for h in range(N_HEADS_A):
                cols = pl.ds(h * tq, tq)
                acc_ref[:, cols] = alphas[h] * acc_ref[:, cols] + pv[:, h * tq:(h + 1) * tq]
            return carry

        lax.fori_loop(0, n_t, body, 0)

    for pair in range(N_HEADS_A // 2):
        outs = []
        for h in (2 * pair, 2 * pair + 1):
            a = acc_ref[:, pl.ds(h * tq, tq)]
            outs.append(a[HEAD_DIM:] / a[0:1])
        o_ref[:, pair * LANES:(pair + 1) * LANES] = jnp.concatenate(outs, axis=0).T.astype(o_ref.dtype)


def _dsa_attention(proj, wi, *, tq=4 * BLOCK_Q, tk=512):
    b, s, _ = proj.shape
    topk = min(TOPK_MAX, s // 4)
    idx_bits = int(np.log2(s))
    assert 2 ** idx_bits == s and s < 2 ** 15 and s % tk == 0 and tk % tq == 0
    qa_w = N_HEADS_A * HEAD_DIM
    qi_w = IDX_HEADS * IDX_DIM
    resident = dict(pipeline_mode=pl.Buffered(1))
    return pl.pallas_call(
        functools.partial(_dsa_kernel, tq=tq, tk=tk, topk=topk, idx_bits=idx_bits, seq=s),
        grid=(b, s // tq),
        in_specs=[
            pl.BlockSpec((None, tq, qa_w), lambda bi, i: (bi, i, COL_QA // qa_w)),
            pl.BlockSpec((None, tq, qi_w), lambda bi, i: (bi, i, COL_QI // qi_w)),
            pl.BlockSpec((None, s, LANES), lambda bi, i: (bi, 0, COL_KVA // LANES), **resident),
            pl.BlockSpec((None, s, LANES), lambda bi, i: (bi, 0, COL_KI // LANES), **resident),
            pl.BlockSpec((None, tq, LANES), lambda bi, i: (bi, i, 0)),
        ],
        out_specs=pl.BlockSpec((None, tq, qa_w), lambda bi, i: (bi, i, 0)),
        out_shape=jax.ShapeDtypeStruct((b, s, qa_w), BF16),
        scratch_shapes=[
            pltpu.VMEM((s, tq), I16),
            pltpu.VMEM((s, tq), I16),
            pltpu.VMEM((s, tq), I16),
            pltpu.VMEM((LANES, s), BF16),
            pltpu.VMEM((LANES, N_HEADS_A * tq), BF16),
            pltpu.VMEM((LANES, IDX_HEADS * tq), BF16),
            pltpu.VMEM((SUBLANES, tq), F32),
            pltpu.VMEM((N_HEADS_A, 1, tq), F32),
            pltpu.VMEM((LANES, N_HEADS_A * tq), F32),
            pltpu.VMEM((1, LANES), F32),
        ],
        compiler_params=_params("parallel", "arbitrary", vmem=DSA_VMEM_LIMIT),
        name="dsa_attention",
    )(proj, proj, proj, proj, wi)


def _sb_kernel(q_ref, k_ref, v_ref, u_ref, o_ref, c_ref, acc_ref, *, tq, tk):
    i = pl.program_id(1)
    q0 = i * tq
    n_t = (q0 + tq - 1) // tk + 1
    n_pairs = N_HEADS_B // 2
    lane = lax.broadcasted_iota(I32, (tq, LANES), 1)
    low_half = lane < HEAD_DIM
    zero = jnp.zeros((tq, LANES), BF16)
    qh = []
    for p in range(n_pairs):
        qp = q_ref[:, p * LANES:(p + 1) * LANES]
        qh += [jnp.where(low_half, qp, zero), jnp.where(low_half, zero, qp)]
    row = lax.broadcasted_iota(I32, (tq, tk), 0) + q0
    col0 = lax.broadcasted_iota(I32, (tq, tk), 1)
    upper = u_ref[...]

    c_ref[...] = jnp.zeros(c_ref.shape, F32)
    acc_ref[...] = jnp.zeros(acc_ref.shape, F32)

    def tile(t, masked):
        off = pl.multiple_of(t * tk, tk)
        strict = (col0 + off) < row
        ks = [k_ref[pl.ds(off, tk), p * LANES:(p + 1) * LANES] for p in range(n_pairs)]
        vs = [v_ref[pl.ds(off, tk), p * LANES:(p + 1) * LANES] for p in range(n_pairs)]
        zs = [_dot_nt(qh[h], ks[h // 2]) for h in range(N_HEADS_B)]
        log_hits, fails, sums = [], [], []
        for z in zs:
            neg_abs = pltpu.bitcast(pltpu.bitcast(z, jnp.uint32) | jnp.uint32(0x80000000), F32)
            soft = jnp.log2(1.0 + jnp.exp2(neg_abs))
            log_hit = jnp.minimum(z, 0.0) - soft
            log_fail = log_hit - z
            if masked:
                log_fail = jnp.where(strict, log_fail, 0.0)
            log_hits.append(log_hit)
            fails.append(log_fail.astype(BF16))
            sums.append(jnp.sum(log_fail, axis=1, keepdims=True))
        afters = [_dot(f, upper) for f in fails]
        probs = []
        for h in range(N_HEADS_B):
            a = jnp.exp2(log_hits[h] + (afters[h] + c_ref[h]))
            if masked:
                a = jnp.where(strict, a, 0.0)
            probs.append(a.astype(BF16))
        pvs = [_dot(probs[h], vs[h // 2]) for h in range(N_HEADS_B)]
        for p in range(n_pairs):
            acc_ref[p] += jnp.where(low_half, pvs[2 * p], pvs[2 * p + 1])
        for h in range(N_HEADS_B):
            c_ref[h] += sums[h]

    tile(n_t - 1, True)

    def alive():
        return jnp.max(c_ref[...]) > DEAD_LOG2

    def cond(state):
        j, live = state
        return jnp.logical_and(j < n_t - 1, live)

    def body(state):
        j, _ = state
        tile(n_t - 2 - j, False)
        return j + 1, alive()

    lax.while_loop(cond, body, (jnp.int32(0), alive()))
    for p in range(n_pairs):
        o_ref[:, p * LANES:(p + 1) * LANES] = acc_ref[p].astype(o_ref.dtype)


def _sb_attention(proj, *, tq=2 * BLOCK_Q, tk=256):
    b, s, _ = proj.shape
    assert s % tk == 0 and s % tq == 0
    w = N_HEADS_B * HEAD_DIM
    upper = (np.arange(tk)[:, None] > np.arange(tk)[None, :]).astype(np.float32)
    resident = dict(pipeline_mode=pl.Buffered(1))
    return pl.pallas_call(
        functools.partial(_sb_kernel, tq=tq, tk=tk),
        grid=(b, s // tq),
        in_specs=[
            pl.BlockSpec((None, tq, w), lambda bi, i: (bi, i, COL_QB // w)),
            pl.BlockSpec((None, s, w), lambda bi, i: (bi, 0, COL_KB // w), **resident),
            pl.BlockSpec((None, s, w), lambda bi, i: (bi, 0, COL_VB // w), **resident),
            pl.BlockSpec((tk, tk), lambda bi, i: (0, 0)),
        ],
        out_specs=pl.BlockSpec((None, tq, w), lambda bi, i: (bi, i, 0)),
        out_shape=jax.ShapeDtypeStruct((b, s, w), BF16),
        scratch_shapes=[
            pltpu.VMEM((N_HEADS_B, tq, 1), F32),
            pltpu.VMEM((N_HEADS_B // 2, tq, LANES), F32),
        ],
        compiler_params=_params("parallel", "arbitrary"),
        name="stick_breaking_attention",
    )(proj, proj, proj, jnp.asarray(upper, BF16))


def _gmlp_tail_kernel(u_ref, v_ref, gv_ref, bv_ref, ws_ref, bs_ref, wo_ref, g_ref, h_ref, o_ref,
                      wt_ref, vn_ref, gated_ref, *, tm):
    @pl.when(pl.program_id(0) == 0)
    def _():
        r = lax.broadcasted_iota(I32, (CHUNK, CHUNK), 0)
        c = lax.broadcasted_iota(I32, (CHUNK, CHUNK), 1)
        for g in range(C_GROUPS):
            wt_ref[g] = jnp.where(c <= r, ws_ref[g], 0.0).astype(BF16)

    v = v_ref[...].astype(F32)
    mu = jnp.mean(v, axis=-1, keepdims=True)
    var = jnp.mean(jnp.square(v - mu), axis=-1, keepdims=True)
    vn_ref[...] = (((v - mu) * lax.rsqrt(var + EPS)) * gv_ref[...] + bv_ref[...]).astype(BF16)
    cw = C_WIDTH // C_GROUPS
    chunks = [slice(ch * CHUNK, (ch + 1) * CHUNK) for ch in range(tm // CHUNK)]
    for g in range(C_GROUPS):
        cols = slice(g * cw, (g + 1) * cw)
        sv = _dot(wt_ref[g], jnp.concatenate([vn_ref[rows, cols] for rows in chunks], axis=1))
        for ch, rows in enumerate(chunks):
            mixed = sv[:, ch * cw:(ch + 1) * cw] + bs_ref[g]
            gated_ref[rows, cols] = (u_ref[rows, cols].astype(F32) * mixed).astype(BF16)
    y = _dot(gated_ref[...], wo_ref[...])
    o_ref[...] = h_ref[...] + _rms(y, g_ref[...])


def _gmlp_tail(z, gv, bv, ws, bs_b, wo, g, h, *, tm=512):
    t, d = h.shape
    cw = C_WIDTH // C_GROUPS
    return pl.pallas_call(
        functools.partial(_gmlp_tail_kernel, tm=tm),
        grid=(t // tm,),
        in_specs=[
            pl.BlockSpec((tm, C_WIDTH), lambda i: (i, 0)),
            pl.BlockSpec((tm, C_WIDTH), lambda i: (i, 1)),
            pl.BlockSpec((1, C_WIDTH), lambda i: (0, 0)),
            pl.BlockSpec((1, C_WIDTH), lambda i: (0, 0)),
            pl.BlockSpec((C_GROUPS, CHUNK, CHUNK), lambda i: (0, 0, 0)),
            pl.BlockSpec((C_GROUPS, CHUNK, cw), lambda i: (0, 0, 0)),
            pl.BlockSpec((C_WIDTH, d), lambda i: (0, 0)),
            pl.BlockSpec((1, d), lambda i: (0, 0)),
            pl.BlockSpec((tm, d), lambda i: (i, 0)),
        ],
        out_specs=pl.BlockSpec((tm, d), lambda i: (i, 0)),
        out_shape=jax.ShapeDtypeStruct((t, d), F32),
        scratch_shapes=[
            pltpu.VMEM((C_GROUPS, CHUNK, CHUNK), BF16),
            pltpu.VMEM((tm, C_WIDTH), BF16),
            pltpu.VMEM((tm, C_WIDTH), BF16),
        ],
        compiler_params=_params("arbitrary"),
        name="gmlp_tail",
    )(z, z, gv, bv, ws, bs_b, wo, g, h)


def _prep_even_weights(w):
    split_points = np.cumsum(EVEN_WIDTHS)[:-1].tolist()
    qa, ka, va, qi, ki, wi, qb, kb, vb = jnp.split(w, split_points, axis=1)
    d = w.shape[0]
    main = jnp.concatenate(
        [qa * (HEAD_DIM ** -0.5), qi * (IDX_DIM ** -0.5), ka, va, ki, jnp.zeros((d, LANES - IDX_DIM), w.dtype),
         qb * (HEAD_DIM ** -0.5 * LOG2_E), kb, vb], axis=1).astype(BF16)
    wi_cols = jnp.concatenate(
        [wi * (IDX_HEADS ** -0.5), jnp.zeros((d, LANES - IDX_HEADS), w.dtype)], axis=1).astype(BF16)
    return main, wi_cols


def kernel(x, norm_gains, w_in_even, w_out_even, w_in_odd, spatial_w, spatial_b, v_norm_gain, v_norm_bias,
           w_out_odd, w_ffn_in, w_ffn_out):
    b, s, d = x.shape
    t = b * s
    depth = norm_gains.shape[0]
    h = x.reshape(t, d)
    tm = 512
    mix_a = N_HEADS_A * HEAD_DIM
    for layer in range(depth):
        g = norm_gains[layer][:, None, :]
        i = layer // 2
        if layer % 2 == 0:
            w_main, w_wi = _prep_even_weights(w_in_even[i])
            proj, wi = _norm_matmul(h, g[0], w_main, act="none", tm=tm, tn=EVEN_COLS // 5, side_w=w_wi)
            proj = proj.reshape(b, s, EVEN_COLS)
            ya = _dsa_attention(proj, wi.reshape(b, s, LANES))
            yb = _sb_attention(proj)
            wo = w_out_even[i].astype(BF16)
            h = _mm_norm_res([ya.reshape(t, mix_a), yb.reshape(t, -1)], [wo[:mix_a], wo[mix_a:]], g[1], h, tm=tm)
        else:
            z = _norm_matmul(h, g[0], w_in_odd[i].astype(BF16), act="gelu", tm=tm, tn=1024)
            bs_b = jnp.broadcast_to(spatial_b[i][:, :, None], (C_GROUPS, CHUNK, C_WIDTH // C_GROUPS))
            h = _gmlp_tail(z, v_norm_gain[i][None, :], v_norm_bias[i][None, :], spatial_w[i], bs_b,
                           w_out_odd[i].astype(BF16), g[1], h)
        act = _norm_matmul(h, g[2], w_ffn_in[layer].astype(BF16), act="swiglu", tm=2 * tm, tn=D_FF // 11)
        h = _mm_norm_res([act], [w_ffn_out[layer].astype(BF16)], g[3], h, tm=2 * tm)
    return h.reshape(b, s, d)
```

```python
import functools

import numpy as np
import jax
import jax.numpy as jnp
from jax import lax
from jax.experimental import pallas as pl
from jax.experimental.pallas import tpu as pltpu

F32 = jnp.float32
BF16 = jnp.bfloat16
I32 = jnp.int32
I16 = jnp.int16

D_MODEL = 1024
HEAD_DIM = 64
N_HEADS_A = 8
N_HEADS_B = 8
IDX_HEADS = 4
IDX_DIM = 64
TOPK_MAX = 256
BLOCK_Q = 128
CHUNK = 128
C_WIDTH = 2048
C_GROUPS = 16
D_FF = 2816
EPS = 1e-6
EVEN_WIDTHS = (N_HEADS_A * HEAD_DIM, HEAD_DIM, HEAD_DIM, IDX_HEADS * IDX_DIM, IDX_DIM, IDX_HEADS,
               N_HEADS_B * HEAD_DIM, N_HEADS_B * HEAD_DIM, N_HEADS_B * HEAD_DIM)

LANES = 128
SUBLANES = 8
PACKED = 16
ROW_BLOCK = 512
FFN_ROW_BLOCK = 1024
DSA_QUERY_BLOCK = 512
DSA_KEY_TILE = 512
SB_QUERY_BLOCK = 256
SB_KEY_TILE = 256
MAIN_TILES = 4
LOGIT_LIMIT = 40.0
VMEM_LIMIT = 48 * 1024 * 1024
DSA_VMEM_LIMIT = 56 * 1024 * 1024

COL_QA = 0
COL_QI = 512
COL_KVA = 768
COL_KI = 896
COL_QB = 1024
COL_KB = 1536
COL_VB = 2048
EVEN_COLS = 2560

LOG2_E = float(np.log2(np.e))
DEAD_LOG2 = -160.0
MASKED = -1e30
NEG_INF_KEY = -2139095041


def _params(*sem, vmem=VMEM_LIMIT):
    return pltpu.CompilerParams(dimension_semantics=sem, vmem_limit_bytes=vmem)


def _dot(a, b):
    return jnp.dot(a, b, preferred_element_type=F32)


def _dot_nt(a, b):
    return lax.dot_general(a, b, (((1,), (1,)), ((), ())), preferred_element_type=F32)


def _rms(y, g):
    ms = jnp.mean(y * y, axis=-1, keepdims=True)
    return (y * lax.rsqrt(ms + EPS)) * g


def _gelu_tanh(x):
    c = np.float32(np.sqrt(2.0 / np.pi))
    return x * (0.5 * (1.0 + jnp.tanh(c * (x + 0.044715 * (x * x * x)))))


def _norm_matmul_kernel(x_ref, g_ref, w_ref, *refs, act, tn, n_side):
    o_ref = refs[n_side]
    xn = _rms(x_ref[...], g_ref[...]).astype(BF16)
    n_out = o_ref.shape[1]
    for j in range(n_out // tn):
        y = _dot(xn, w_ref[:, j * tn:(j + 1) * tn])
        if act == "gelu":
            y = _gelu_tanh(y)
        elif act == "swiglu":
            up = _dot(xn, w_ref[:, n_out + j * tn:n_out + (j + 1) * tn])
            y = (y / (1.0 + jnp.exp(-y))) * up
        o_ref[:, j * tn:(j + 1) * tn] = y.astype(o_ref.dtype)
    if n_side:
        refs[n_side + 1][...] = _dot(xn, refs[0][...])


def _norm_matmul(x, g, w, *, act, tm, tn, side_w=None):
    t, d = x.shape
    n_out = w.shape[1] // 2 if act == "swiglu" else w.shape[1]
    n_side = 0 if side_w is None else 1
    resident = dict(pipeline_mode=pl.Buffered(1))
    in_specs = [pl.BlockSpec((tm, d), lambda i: (i, 0)),
                pl.BlockSpec((1, d), lambda i: (0, 0)),
                pl.BlockSpec(w.shape, lambda i: (0, 0), **resident)]
    out_specs = [pl.BlockSpec((tm, n_out), lambda i: (i, 0))]
    out_shape = [jax.ShapeDtypeStruct((t, n_out), BF16)]
    args = [x, g, w]
    if n_side:
        in_specs.append(pl.BlockSpec(side_w.shape, lambda i: (0, 0), **resident))
        out_specs.append(pl.BlockSpec((tm, side_w.shape[1]), lambda i: (i, 0)))
        out_shape.append(jax.ShapeDtypeStruct((t, side_w.shape[1]), F32))
        args.append(side_w)
    outs = pl.pallas_call(
        functools.partial(_norm_matmul_kernel, act=act, tn=tn, n_side=n_side),
        grid=(t // tm,),
        in_specs=in_specs,
        out_specs=out_specs,
        out_shape=out_shape,
        compiler_params=_params("parallel"),
        name="norm_matmul_" + act,
    )(*args)
    return outs if n_side else outs[0]


def _mm_norm_res_kernel(*refs, n_in):
    x_refs, w_refs = refs[:n_in], refs[n_in:2 * n_in]
    g_ref, h_ref, o_ref = refs[2 * n_in:]
    y = _dot(x_refs[0][...], w_refs[0][...])
    for k in range(1, n_in):
        y = y + _dot(x_refs[k][...], w_refs[k][...])
    o_ref[...] = h_ref[...] + _rms(y, g_ref[...])


def _mm_norm_res(xs, ws, g, h, *, tm):
    t, d = h.shape
    n_in = len(xs)
    in_specs = [pl.BlockSpec((tm, x.shape[1]), lambda i: (i, 0)) for x in xs]
    in_specs += [pl.BlockSpec(w.shape, lambda i: (0, 0)) for w in ws]
    in_specs += [pl.BlockSpec((1, d), lambda i: (0, 0)), pl.BlockSpec((tm, d), lambda i: (i, 0))]
    return pl.pallas_call(
        functools.partial(_mm_norm_res_kernel, n_in=n_in),
        grid=(t // tm,),
        in_specs=in_specs,
        out_specs=pl.BlockSpec((tm, d), lambda i: (i, 0)),
        out_shape=jax.ShapeDtypeStruct((t, d), F32),
        compiler_params=_params("parallel"),
        name="matmul_norm_residual",
    )(*xs, *ws, g, h)


def _dsa_kernel(q_ref, qi_ref, kv_ref, ki_ref, wi_ref, o_ref,
                hi_ref, lo_ref, tie_ref, kvt_ref, rq_ref, rqi_ref, w_ref, m_ref, acc_ref, kmax_ref,
                *, tq, tk, topk, seq):
    i = pl.program_id(1)
    q0 = i * tq
    n_t = q0 // tk + 1
    half_rows = lax.broadcasted_iota(I32, (LANES, tq), 0) < HEAD_DIM

    @pl.when(i == 0)
    def _():
        feat_is_k = lax.broadcasted_iota(I32, (LANES, LANES), 0) < HEAD_DIM

        def blk(j, kmax2):
            off = pl.multiple_of(j * LANES, LANES)
            t = kv_ref[pl.ds(off, LANES), :].astype(F32).T
            kvt_ref[:, pl.ds(off, LANES)] = jnp.where(feat_is_k, 1.0, t).astype(BF16)
            return jnp.maximum(kmax2, jnp.sum(jnp.where(feat_is_k, t * t, 0.0), axis=0, keepdims=True))

        kmax2 = lax.fori_loop(0, seq // LANES, blk, jnp.zeros((1, LANES), F32))
        kmax_ref[...] = jnp.broadcast_to(jnp.max(kmax2, axis=1, keepdims=True), (1, LANES))

    zeros_half = jnp.zeros((HEAD_DIM, tq), F32)
    q2 = jnp.zeros((1, tq), F32)
    for pair in range(N_HEADS_A // 2):
        t = q_ref[:, pair * LANES:(pair + 1) * LANES].astype(F32).T
        even = jnp.where(half_rows, t, 0.0)
        odd = jnp.concatenate([t[HEAD_DIM:], zeros_half], axis=0)
        rq_ref[:, pl.ds(2 * pair * tq, tq)] = even.astype(BF16)
        rq_ref[:, pl.ds((2 * pair + 1) * tq, tq)] = odd.astype(BF16)
        q2 = jnp.maximum(q2, jnp.sum(t * t, axis=0, keepdims=True))
    for pair in range(IDX_HEADS // 2):
        t = qi_ref[:, pair * LANES:(pair + 1) * LANES].astype(F32).T
        swapped = jnp.concatenate([t[HEAD_DIM:], t[:HEAD_DIM]], axis=0)
        rqi_ref[:, pl.ds(2 * pair * tq, tq)] = t.astype(BF16)
        rqi_ref[:, pl.ds((2 * pair + 1) * tq, tq)] = swapped.astype(BF16)
    w_ref[...] = wi_ref[...].T[:SUBLANES]

    qpos = lax.broadcasted_iota(I32, (tk, tq), 1) + q0
    krow = lax.broadcasted_iota(I32, (tk, tq), 0)

    def score_body(t, carry):
        off = pl.multiple_of(t * tk, tk)
        dots = _dot(ki_ref[pl.ds(off, tk), :], rqi_ref[...])
        sc = jnp.zeros((tk, tq), F32)
        for h in range(IDX_HEADS):
            sc = sc + jnp.maximum(dots[:, h * tq:(h + 1) * tq], 0.0) * w_ref[h:h + 1, :]
        sc = jnp.where(krow + off <= qpos, sc, -jnp.inf)
        b = pltpu.bitcast(sc, I32)
        key = b ^ ((b >> 31) & 0x7FFFFFFF)
        key = jnp.where(key == -1, 0, key)
        hi_ref[pl.ds(off, tk), :] = (key >> 16).astype(I16)
        lo_ref[pl.ds(off, tk), :] = (key ^ 0x8000).astype(I16)
        return carry

    lax.fori_loop(0, n_t, score_body, 0)

    n_acc = 4
    one, zero = jnp.ones((PACKED, tq), BF16), jnp.zeros((PACKED, tq), BF16)

    n_main = n_t // MAIN_TILES
    n_tail = n_t - n_main * MAIN_TILES

    def count(ref, pred):
        def body(width, base, t, accs):
            accs = list(accs)
            tile = ref[pl.ds(pl.multiple_of(base + t * width, tk), width), :]
            for r in range(width // PACKED):
                x = tile[r * PACKED:(r + 1) * PACKED]
                accs[r % n_acc] = accs[r % n_acc] + jnp.where(pred(x), one, zero)
            return tuple(accs)
        accs = lax.fori_loop(0, n_main, functools.partial(body, MAIN_TILES * tk, 0), (zero,) * n_acc)
        accs = lax.fori_loop(0, n_tail, functools.partial(body, tk, n_main * MAIN_TILES * tk), accs)
        total = accs[0].astype(F32)
        for a in accs[1:]:
            total = total + a.astype(F32)
        return jnp.sum(total, axis=0, keepdims=True)

    def largest16(ref, want):
        c = count(ref, lambda x: x >= jnp.int16(0))
        ok = c >= want
        v = jnp.broadcast_to(jnp.where(ok, 0, -32768), (PACKED, tq)).astype(I32)
        above = jnp.where(ok, 0.0, c)

        def body(p, carry):
            v, above = carry
            cand = v | jnp.left_shift(jnp.int32(1), 14 - p)
            c16 = cand.astype(I16)
            c = count(ref, lambda x: x >= c16)
            ok = c >= want
            return jnp.where(ok, cand, v), jnp.where(ok, above, c)

        return lax.fori_loop(0, 15, body, (v, above))

    kf = np.float32(topk)
    thr_hi, above_hi = largest16(hi_ref, kf)
    thr_hi16 = thr_hi.astype(I16)
    thr_hi_tile = jnp.broadcast_to(thr_hi16[0:1], (tk, tq))

    def low_body(t, carry):
        rows = pl.ds(pl.multiple_of(t * tk, tk), tk)
        lo_ref[rows, :] = jnp.where(hi_ref[rows, :] == thr_hi_tile, lo_ref[rows, :], jnp.int16(-32768))
        return carry

    lax.fori_loop(0, n_t, low_body, 0)
    thr_lo, above_lo = largest16(lo_ref, kf - above_hi)
    thr_lo16 = thr_lo.astype(I16)
    thr = (thr_hi[0:1] << 16) | ((thr_lo[0:1] + 32768) & 0xFFFF)

    need = kf - above_hi - above_lo
    none = jnp.int16(32767)
    dead = jnp.broadcast_to(jnp.where(thr == NEG_INF_KEY, 32767, 0), (tk, tq)).astype(I16)
    thr_lo_tile = jnp.broadcast_to(thr_lo16[0:1], (tk, tq))

    def tie_body(t, carry):
        off = pl.multiple_of(t * tk, tk)
        rows = pl.ds(off, tk)
        pos = (krow + off).astype(I16) | dead
        at_lo = jnp.where(lo_ref[rows, :] == thr_lo_tile, pos, none)
        tie_ref[rows, :] = jnp.where(hi_ref[rows, :] == thr_hi_tile, at_lo, none)
        return carry

    lax.fori_loop(0, n_t, tie_body, 0)

    pos_bits = 32 - lax.clz(n_t * tk - 1)

    def cut_body(p, cut):
        cand = cut | jnp.left_shift(jnp.int32(1), pos_bits - 1 - p)
        c16 = cand.astype(I16)
        return jnp.where(count(tie_ref, lambda x: x < c16) < need, cand, cut)

    cut = lax.fori_loop(0, pos_bits, cut_body, jnp.zeros((PACKED, tq), I32))
    cut_row = jnp.where(thr != NEG_INF_KEY, cut[0:1], -1)

    acc_ref[...] = jnp.zeros(acc_ref.shape, F32)
    bound2 = jnp.max(q2, axis=1, keepdims=True) * kmax_ref[:, 0:1]
    safe = jnp.max(bound2) <= LOGIT_LIMIT * LOGIT_LIMIT

    one_t, zero_t = jnp.ones((tk, tq), BF16), jnp.zeros((tk, tq), BF16)
    cut_tile = jnp.broadcast_to(cut_row, (tk, tq)).astype(I16)

    def keep_of(rows):
        tied = jnp.where(tie_ref[rows, :] <= cut_tile, one_t, zero_t)
        return jnp.where(hi_ref[rows, :] > thr_hi_tile, one_t,
                         jnp.where(lo_ref[rows, :] > thr_lo_tile, one_t, tied))

    @pl.when(safe)
    def _():
        def body(t, carry):
            rows = pl.ds(pl.multiple_of(t * tk, tk), tk)
            keep = keep_of(rows)
            s_all = _dot(kv_ref[rows, :], rq_ref[...])
            probs = [jnp.exp(s_all[:, h * tq:(h + 1) * tq]).astype(BF16) * keep for h in range(N_HEADS_A)]
            acc_ref[...] += _dot(kvt_ref[:, rows], jnp.concatenate(probs, axis=1))
            return carry

        lax.fori_loop(0, n_t, body, 0)

    @pl.when(jnp.logical_not(safe))
    def _():
        m_ref[...] = jnp.full(m_ref.shape, MASKED, F32)

        def body(t, carry):
            off = pl.multiple_of(t * tk, tk)
            bias = (keep_of(pl.ds(off, tk)).astype(F32) - 1.0) * -MASKED
            s_all = _dot(kv_ref[pl.ds(off, tk), :], rq_ref[...])
            probs, alphas = [], []
            for h in range(N_HEADS_A):
                s = s_all[:, h * tq:(h + 1) * tq] + bias
                m_old = m_ref[h]
                m_new = jnp.maximum(m_old, jnp.max(s, axis=0, keepdims=True))
                probs.append(jnp.exp(s - m_new).astype(BF16))
                alphas.append(jnp.exp(m_old - m_new))
                m_ref[h] = m_new
            pv = _dot(kvt_ref[:, pl.ds(off, tk)], jnp.concatenate(probs, axis=1))
            for h in range(N_HEADS_A):
                cols = pl.ds(h * tq, tq)
                acc_ref[:, cols] = alphas[h] * acc_ref[:, cols] + pv[:, h * tq:(h + 1) * tq]
            return carry

        lax.fori_loop(0, n_t, body, 0)

    for pair in range(N_HEADS_A // 2):
        outs = []
        for h in (2 * pair, 2 * pair + 1):
            a = acc_ref[:, pl.ds(h * tq, tq)]
            outs.append(a[HEAD_DIM:] / a[0:1])
        o_ref[:, pair * LANES:(pair + 1) * LANES] = jnp.concatenate(outs, axis=0).T.astype(o_ref.dtype)


def _dsa_attention(proj, wi, *, tq=DSA_QUERY_BLOCK, tk=DSA_KEY_TILE):
    b, s, _ = proj.shape
    topk = min(TOPK_MAX, s // 4)
    assert s < 2 ** 15 and s % tk == 0 and tk % tq == 0
    qa_w = N_HEADS_A * HEAD_DIM
    qi_w = IDX_HEADS * IDX_DIM
    resident = dict(pipeline_mode=pl.Buffered(1))
    return pl.pallas_call(
        functools.partial(_dsa_kernel, tq=tq, tk=tk, topk=topk, seq=s),
        grid=(b, s // tq),
        in_specs=[
            pl.BlockSpec((None, tq, qa_w), lambda bi, i: (bi, i, COL_QA // qa_w)),
            pl.BlockSpec((None, tq, qi_w), lambda bi, i: (bi, i, COL_QI // qi_w)),
            pl.BlockSpec((None, s, LANES), lambda bi, i: (bi, 0, COL_KVA // LANES), **resident),
            pl.BlockSpec((None, s, LANES), lambda bi, i: (bi, 0, COL_KI // LANES), **resident),
            pl.BlockSpec((None, tq, LANES), lambda bi, i: (bi, i, 0)),
        ],
        out_specs=pl.BlockSpec((None, tq, qa_w), lambda bi, i: (bi, i, 0)),
        out_shape=jax.ShapeDtypeStruct((b, s, qa_w), BF16),
        scratch_shapes=[
            pltpu.VMEM((s, tq), I16),
            pltpu.VMEM((s, tq), I16),
            pltpu.VMEM((s, tq), I16),
            pltpu.VMEM((LANES, s), BF16),
            pltpu.VMEM((LANES, N_HEADS_A * tq), BF16),
            pltpu.VMEM((LANES, IDX_HEADS * tq), BF16),
            pltpu.VMEM((SUBLANES, tq), F32),
            pltpu.VMEM((N_HEADS_A, 1, tq), F32),
            pltpu.VMEM((LANES, N_HEADS_A * tq), F32),
            pltpu.VMEM((1, LANES), F32),
        ],
        compiler_params=_params("parallel", "arbitrary", vmem=DSA_VMEM_LIMIT),
        name="dsa_attention",
    )(proj, proj, proj, proj, wi)


def _sb_kernel(q_ref, k_ref, v_ref, u_ref, o_ref, c_ref, acc_ref, *, tq, tk):
    i = pl.program_id(1)
    q0 = i * tq
    n_t = (q0 + tq - 1) // tk + 1
    n_pairs = N_HEADS_B // 2
    lane = lax.broadcasted_iota(I32, (tq, LANES), 1)
    low_half = lane < HEAD_DIM
    zero = jnp.zeros((tq, LANES), BF16)
    qh = []
    for p in range(n_pairs):
        qp = q_ref[:, p * LANES:(p + 1) * LANES]
        qh += [jnp.where(low_half, qp, zero), jnp.where(low_half, zero, qp)]
    row = lax.broadcasted_iota(I32, (tq, tk), 0) + q0
    col0 = lax.broadcasted_iota(I32, (tq, tk), 1)
    upper = u_ref[...]

    c_ref[...] = jnp.zeros(c_ref.shape, F32)
    acc_ref[...] = jnp.zeros(acc_ref.shape, F32)

    def tile(t, masked):
        off = pl.multiple_of(t * tk, tk)
        strict = (col0 + off) < row
        ks = [k_ref[pl.ds(off, tk), p * LANES:(p + 1) * LANES] for p in range(n_pairs)]
        vs = [v_ref[pl.ds(off, tk), p * LANES:(p + 1) * LANES] for p in range(n_pairs)]
        zs = [_dot_nt(qh[h], ks[h // 2]) for h in range(N_HEADS_B)]
        log_hits, fails, sums = [], [], []
        for z in zs:
            neg_abs = pltpu.bitcast(pltpu.bitcast(z, jnp.uint32) | jnp.uint32(0x80000000), F32)
            soft = jnp.log2(1.0 + jnp.exp2(neg_abs))
            log_hit = jnp.minimum(z, 0.0) - soft
            log_fail = log_hit - z
            if masked:
                log_fail = jnp.where(strict, log_fail, 0.0)
            log_hits.append(log_hit)
            fails.append(log_fail.astype(BF16))
            sums.append(jnp.sum(log_fail, axis=1, keepdims=True))
        afters = [_dot(f, upper) for f in fails]
        probs = []
        for h in range(N_HEADS_B):
            a = jnp.exp2(log_hits[h] + (afters[h] + c_ref[h]))
            if masked:
                a = jnp.where(strict, a, 0.0)
            probs.append(a.astype(BF16))
        pvs = [_dot(probs[h], vs[h // 2]) for h in range(N_HEADS_B)]
        for p in range(n_pairs):
            acc_ref[p] += jnp.where(low_half, pvs[2 * p], pvs[2 * p + 1])
        for h in range(N_HEADS_B):
            c_ref[h] += sums[h]

    tile(n_t - 1, True)

    def alive():
        return jnp.max(c_ref[...]) > DEAD_LOG2

    def cond(state):
        j, live = state
        return jnp.logical_and(j < n_t - 1, live)

    def body(state):
        j, _ = state
        tile(n_t - 2 - j, False)
        return j + 1, alive()

    lax.while_loop(cond, body, (jnp.int32(0), alive()))
    for p in range(n_pairs):
        o_ref[:, p * LANES:(p + 1) * LANES] = acc_ref[p].astype(o_ref.dtype)


def _sb_attention(proj, *, tq=SB_QUERY_BLOCK, tk=SB_KEY_TILE):
    b, s, _ = proj.shape
    assert s % tk == 0 and s % tq == 0
    w = N_HEADS_B * HEAD_DIM
    upper = (np.arange(tk)[:, None] > np.arange(tk)[None, :]).astype(np.float32)
    resident = dict(pipeline_mode=pl.Buffered(1))
    return pl.pallas_call(
        functools.partial(_sb_kernel, tq=tq, tk=tk),
        grid=(b, s // tq),
        in_specs=[
            pl.BlockSpec((None, tq, w), lambda bi, i: (bi, i, COL_QB // w)),
            pl.BlockSpec((None, s, w), lambda bi, i: (bi, 0, COL_KB // w), **resident),
            pl.BlockSpec((None, s, w), lambda bi, i: (bi, 0, COL_VB // w), **resident),
            pl.BlockSpec((tk, tk), lambda bi, i: (0, 0)),
        ],
        out_specs=pl.BlockSpec((None, tq, w), lambda bi, i: (bi, i, 0)),
        out_shape=jax.ShapeDtypeStruct((b, s, w), BF16),
        scratch_shapes=[
            pltpu.VMEM((N_HEADS_B, tq, 1), F32),
            pltpu.VMEM((N_HEADS_B // 2, tq, LANES), F32),
        ],
        compiler_params=_params("parallel", "arbitrary"),
        name="stick_breaking_attention",
    )(proj, proj, proj, jnp.asarray(upper, BF16))


def _gmlp_tail_kernel(u_ref, v_ref, gv_ref, bv_ref, ws_ref, bs_ref, wo_ref, g_ref, h_ref, o_ref,
                      wt_ref, vn_ref, gated_ref, *, tm):
    @pl.when(pl.program_id(0) == 0)
    def _():
        r = lax.broadcasted_iota(I32, (CHUNK, CHUNK), 0)
        c = lax.broadcasted_iota(I32, (CHUNK, CHUNK), 1)
        for g in range(C_GROUPS):
            wt_ref[g] = jnp.where(c <= r, ws_ref[g], 0.0).astype(BF16)

    v = v_ref[...].astype(F32)
    mu = jnp.mean(v, axis=-1, keepdims=True)
    var = jnp.mean(jnp.square(v - mu), axis=-1, keepdims=True)
    vn_ref[...] = (((v - mu) * lax.rsqrt(var + EPS)) * gv_ref[...] + bv_ref[...]).astype(BF16)
    cw = C_WIDTH // C_GROUPS
    chunks = [slice(ch * CHUNK, (ch + 1) * CHUNK) for ch in range(tm // CHUNK)]
    for g in range(C_GROUPS):
        cols = slice(g * cw, (g + 1) * cw)
        sv = _dot(wt_ref[g], jnp.concatenate([vn_ref[rows, cols] for rows in chunks], axis=1))
        for ch, rows in enumerate(chunks):
            mixed = sv[:, ch * cw:(ch + 1) * cw] + bs_ref[g]
            gated_ref[rows, cols] = (u_ref[rows, cols].astype(F32) * mixed).astype(BF16)
    y = _dot(gated_ref[...], wo_ref[...])
    o_ref[...] = h_ref[...] + _rms(y, g_ref[...])


def _gmlp_tail(z, gv, bv, ws, bs_b, wo, g, h, *, tm=ROW_BLOCK):
    t, d = h.shape
    cw = C_WIDTH // C_GROUPS
    return pl.pallas_call(
        functools.partial(_gmlp_tail_kernel, tm=tm),
        grid=(t // tm,),
        in_specs=[
            pl.BlockSpec((tm, C_WIDTH), lambda i: (i, 0)),
            pl.BlockSpec((tm, C_WIDTH), lambda i: (i, 1)),
            pl.BlockSpec((1, C_WIDTH), lambda i: (0, 0)),
            pl.BlockSpec((1, C_WIDTH), lambda i: (0, 0)),
            pl.BlockSpec((C_GROUPS, CHUNK, CHUNK), lambda i: (0, 0, 0)),
            pl.BlockSpec((C_GROUPS, CHUNK, cw), lambda i: (0, 0, 0)),
            pl.BlockSpec((C_WIDTH, d), lambda i: (0, 0)),
            pl.BlockSpec((1, d), lambda i: (0, 0)),
            pl.BlockSpec((tm, d), lambda i: (i, 0)),
        ],
        out_specs=pl.BlockSpec((tm, d), lambda i: (i, 0)),
        out_shape=jax.ShapeDtypeStruct((t, d), F32),
        scratch_shapes=[
            pltpu.VMEM((C_GROUPS, CHUNK, CHUNK), BF16),
            pltpu.VMEM((tm, C_WIDTH), BF16),
            pltpu.VMEM((tm, C_WIDTH), BF16),
        ],
        compiler_params=_params("arbitrary"),
        name="gmlp_tail",
    )(z, z, gv, bv, ws, bs_b, wo, g, h)


def _prep_even_weights(w):
    split_points = np.cumsum(EVEN_WIDTHS)[:-1].tolist()
    qa, ka, va, qi, ki, wi, qb, kb, vb = jnp.split(w, split_points, axis=1)
    d = w.shape[0]
    main = jnp.concatenate(
        [qa * (HEAD_DIM ** -0.5), qi * (IDX_DIM ** -0.5), ka, va, ki, jnp.zeros((d, LANES - IDX_DIM), w.dtype),
         qb * (HEAD_DIM ** -0.5 * LOG2_E), kb, vb], axis=1).astype(BF16)
    wi_cols = jnp.concatenate(
        [wi * (IDX_HEADS ** -0.5), jnp.zeros((d, LANES - IDX_HEADS), w.dtype)], axis=1).astype(BF16)
    return main, wi_cols


def kernel(x, norm_gains, w_in_even, w_out_even, w_in_odd, spatial_w, spatial_b, v_norm_gain, v_norm_bias,
           w_out_odd, w_ffn_in, w_ffn_out):
    b, s, d = x.shape
    t = b * s
    depth = norm_gains.shape[0]
    h = x.reshape(t, d)
    tm = ROW_BLOCK
    mix_a = N_HEADS_A * HEAD_DIM
    for layer in range(depth):
        g = norm_gains[layer][:, None, :]
        i = layer // 2
        if layer % 2 == 0:
            w_main, w_wi = _prep_even_weights(w_in_even[i])
            proj, wi = _norm_matmul(h, g[0], w_main, act="none", tm=tm, tn=EVEN_COLS // 5, side_w=w_wi)
            proj = proj.reshape(b, s, EVEN_COLS)
            ya = _dsa_attention(proj, wi.reshape(b, s, LANES))
            yb = _sb_attention(proj)
            wo = w_out_even[i].astype(BF16)
            h = _mm_norm_res([ya.reshape(t, mix_a), yb.reshape(t, -1)], [wo[:mix_a], wo[mix_a:]], g[1], h, tm=tm)
        else:
            z = _norm_matmul(h, g[0], w_in_odd[i].astype(BF16), act="gelu", tm=tm, tn=1024)
            bs_b = jnp.broadcast_to(spatial_b[i][:, :, None], (C_GROUPS, CHUNK, C_WIDTH // C_GROUPS))
            h = _gmlp_tail(z, v_norm_gain[i][None, :], v_norm_bias[i][None, :], spatial_w[i], bs_b,
                           w_out_odd[i].astype(BF16), g[1], h)
        act = _norm_matmul(h, g[2], w_ffn_in[layer].astype(BF16), act="swiglu", tm=FFN_ROW_BLOCK, tn=D_FF // 11)
        h = _mm_norm_res([act], [w_ffn_out[layer].astype(BF16)], g[3], h, tm=FFN_ROW_BLOCK)
    return h.reshape(b, s, d)
```

```python
import functools

import numpy as np
import jax
import jax.numpy as jnp
from jax import lax
from jax.experimental import pallas as pl
from jax.experimental.pallas import tpu as pltpu

F32 = jnp.float32
BF16 = jnp.bfloat16
I32 = jnp.int32
I16 = jnp.int16

D_MODEL = 1024
HEAD_DIM = 64
N_HEADS_A = 8
N_HEADS_B = 8
IDX_HEADS = 4
IDX_DIM = 64
TOPK_MAX = 256
BLOCK_Q = 128
CHUNK = 128
C_WIDTH = 2048
C_GROUPS = 16
D_FF = 2816
EPS = 1e-6
EVEN_WIDTHS = (N_HEADS_A * HEAD_DIM, HEAD_DIM, HEAD_DIM, IDX_HEADS * IDX_DIM, IDX_DIM, IDX_HEADS,
               N_HEADS_B * HEAD_DIM, N_HEADS_B * HEAD_DIM, N_HEADS_B * HEAD_DIM)

LANES = 128
SUBLANES = 8
PACKED = 16
ROW_BLOCK = 512
DSA_QUERY_BLOCK = 512
DSA_KEY_TILE = 512
SB_QUERY_BLOCK = 256
SB_KEY_TILE = 256
MAIN_TILES = 4
LOGIT_LIMIT = 40.0
VMEM_LIMIT = 48 * 1024 * 1024
DSA_VMEM_LIMIT = 56 * 1024 * 1024

COL_QA = 0
COL_QI = 512
COL_KVA = 768
COL_KI = 896
COL_QB = 1024
COL_KB = 1536
COL_VB = 2048
EVEN_COLS = 2560

LOG2_E = float(np.log2(np.e))
DEAD_LOG2 = -160.0
MASKED = -1e30
NEG_INF_KEY = -2139095041


def _params(*sem, vmem=VMEM_LIMIT):
    return pltpu.CompilerParams(dimension_semantics=sem, vmem_limit_bytes=vmem)


def _dot(a, b):
    return jnp.dot(a, b, preferred_element_type=F32)


def _dot_nt(a, b):
    return lax.dot_general(a, b, (((1,), (1,)), ((), ())), preferred_element_type=F32)


def _rms(y, g):
    ms = jnp.mean(y * y, axis=-1, keepdims=True)
    return (y * lax.rsqrt(ms + EPS)) * g


def _gelu_tanh(x):
    c = np.float32(np.sqrt(2.0 / np.pi))
    return x * (0.5 * (1.0 + jnp.tanh(c * (x + 0.044715 * (x * x * x)))))


def _norm_matmul_kernel(x_ref, g_ref, w_ref, *refs, act, tn, n_side):
    o_ref = refs[n_side]
    xn = _rms(x_ref[...], g_ref[...]).astype(BF16)
    n_out = o_ref.shape[1]
    for j in range(n_out // tn):
        y = _dot(xn, w_ref[:, j * tn:(j + 1) * tn])
        if act == "gelu":
            y = _gelu_tanh(y)
        o_ref[:, j * tn:(j + 1) * tn] = y.astype(o_ref.dtype)
    if n_side:
        refs[n_side + 1][...] = _dot(xn, refs[0][...])


def _norm_matmul(x, g, w, *, act, tm, tn, side_w=None):
    t, d = x.shape
    n_out = w.shape[1]
    n_side = 0 if side_w is None else 1
    resident = dict(pipeline_mode=pl.Buffered(1))
    in_specs = [pl.BlockSpec((tm, d), lambda i: (i, 0)),
                pl.BlockSpec((1, d), lambda i: (0, 0)),
                pl.BlockSpec(w.shape, lambda i: (0, 0), **resident)]
    out_specs = [pl.BlockSpec((tm, n_out), lambda i: (i, 0))]
    out_shape = [jax.ShapeDtypeStruct((t, n_out), BF16)]
    args = [x, g, w]
    if n_side:
        in_specs.append(pl.BlockSpec(side_w.shape, lambda i: (0, 0), **resident))
        out_specs.append(pl.BlockSpec((tm, side_w.shape[1]), lambda i: (i, 0)))
        out_shape.append(jax.ShapeDtypeStruct((t, side_w.shape[1]), F32))
        args.append(side_w)
    outs = pl.pallas_call(
        functools.partial(_norm_matmul_kernel, act=act, tn=tn, n_side=n_side),
        grid=(t // tm,),
        in_specs=in_specs,
        out_specs=out_specs,
        out_shape=out_shape,
        compiler_params=_params("parallel"),
        name="norm_matmul_" + act,
    )(*args)
    return outs if n_side else outs[0]


def _mm_norm_res_kernel(*refs, n_in):
    x_refs, w_refs = refs[:n_in], refs[n_in:2 * n_in]
    g_ref, h_ref, o_ref = refs[2 * n_in:]
    y = _dot(x_refs[0][...], w_refs[0][...])
    for k in range(1, n_in):
        y = y + _dot(x_refs[k][...], w_refs[k][...])
    o_ref[...] = h_ref[...] + _rms(y, g_ref[...])


def _mm_norm_res(xs, ws, g, h, *, tm):
    t, d = h.shape
    n_in = len(xs)
    in_specs = [pl.BlockSpec((tm, x.shape[1]), lambda i: (i, 0)) for x in xs]
    in_specs += [pl.BlockSpec(w.shape, lambda i: (0, 0)) for w in ws]
    in_specs += [pl.BlockSpec((1, d), lambda i: (0, 0)), pl.BlockSpec((tm, d), lambda i: (i, 0))]
    return pl.pallas_call(
        functools.partial(_mm_norm_res_kernel, n_in=n_in),
        grid=(t // tm,),
        in_specs=in_specs,
        out_specs=pl.BlockSpec((tm, d), lambda i: (i, 0)),
        out_shape=jax.ShapeDtypeStruct((t, d), F32),
        compiler_params=_params("parallel"),
        name="matmul_norm_residual",
    )(*xs, *ws, g, h)


def _ffn_kernel(h_ref, g_in_ref, w_in_ref, w_out_ref, g_out_ref, o_ref, *, tn):
    h = h_ref[...]
    xn = _rms(h, g_in_ref[...]).astype(BF16)
    d_ff = w_out_ref.shape[0]
    y = None
    for j in range(d_ff // tn):
        gate = _dot(xn, w_in_ref[:, j * tn:(j + 1) * tn])
        up = _dot(xn, w_in_ref[:, d_ff + j * tn:d_ff + (j + 1) * tn])
        act = ((gate / (1.0 + jnp.exp(-gate))) * up).astype(BF16)
        part = _dot(act, w_out_ref[j * tn:(j + 1) * tn, :])
        y = part if y is None else y + part
    o_ref[...] = h + _rms(y, g_out_ref[...])


def _ffn(h, g_in, w_in, w_out, g_out, *, tm, tn):
    t, d = h.shape
    resident = dict(pipeline_mode=pl.Buffered(1))
    return pl.pallas_call(
        functools.partial(_ffn_kernel, tn=tn),
        grid=(t // tm,),
        in_specs=[pl.BlockSpec((tm, d), lambda i: (i, 0)),
                  pl.BlockSpec((1, d), lambda i: (0, 0)),
                  pl.BlockSpec(w_in.shape, lambda i: (0, 0), **resident),
                  pl.BlockSpec(w_out.shape, lambda i: (0, 0), **resident),
                  pl.BlockSpec((1, d), lambda i: (0, 0))],
        out_specs=pl.BlockSpec((tm, d), lambda i: (i, 0)),
        out_shape=jax.ShapeDtypeStruct((t, d), F32),
        compiler_params=_params("parallel"),
        name="ffn_fused",
    )(h, g_in, w_in, w_out, g_out)


def _dsa_kernel(q_ref, qi_ref, kv_ref, ki_ref, wi_ref, o_ref,
                hi_ref, lo_ref, tie_ref, kvt_ref, rq_ref, rqi_ref, w_ref, m_ref, acc_ref, kmax_ref,
                *, tq, tk, topk, seq):
    i = pl.program_id(1)
    q0 = i * tq
    n_t = q0 // tk + 1
    half_rows = lax.broadcasted_iota(I32, (LANES, tq), 0) < HEAD_DIM

    @pl.when(i == 0)
    def _():
        feat_is_k = lax.broadcasted_iota(I32, (LANES, LANES), 0) < HEAD_DIM

        def blk(j, kmax2):
            off = pl.multiple_of(j * LANES, LANES)
            t = kv_ref[pl.ds(off, LANES), :].astype(F32).T
            kvt_ref[:, pl.ds(off, LANES)] = jnp.where(feat_is_k, 1.0, t).astype(BF16)
            return jnp.maximum(kmax2, jnp.sum(jnp.where(feat_is_k, t * t, 0.0), axis=0, keepdims=True))

        kmax2 = lax.fori_loop(0, seq // LANES, blk, jnp.zeros((1, LANES), F32))
        kmax_ref[...] = jnp.broadcast_to(jnp.max(kmax2, axis=1, keepdims=True), (1, LANES))

    zeros_half = jnp.zeros((HEAD_DIM, tq), F32)
    q2 = jnp.zeros((1, tq), F32)
    for pair in range(N_HEADS_A // 2):
        t = q_ref[:, pair * LANES:(pair + 1) * LANES].astype(F32).T
        even = jnp.where(half_rows, t, 0.0)
        odd = jnp.concatenate([t[HEAD_DIM:], zeros_half], axis=0)
        rq_ref[:, pl.ds(2 * pair * tq, tq)] = even.astype(BF16)
        rq_ref[:, pl.ds((2 * pair + 1) * tq, tq)] = odd.astype(BF16)
        q2 = jnp.maximum(q2, jnp.sum(t * t, axis=0, keepdims=True))
    for pair in range(IDX_HEADS // 2):
        t = qi_ref[:, pair * LANES:(pair + 1) * LANES].astype(F32).T
        swapped = jnp.concatenate([t[HEAD_DIM:], t[:HEAD_DIM]], axis=0)
        rqi_ref[:, pl.ds(2 * pair * tq, tq)] = t.astype(BF16)
        rqi_ref[:, pl.ds((2 * pair + 1) * tq, tq)] = swapped.astype(BF16)
    w_ref[...] = wi_ref[...].T[:SUBLANES]

    qpos = lax.broadcasted_iota(I32, (tk, tq), 1) + q0
    krow = lax.broadcasted_iota(I32, (tk, tq), 0)

    def score_body(t, carry):
        off = pl.multiple_of(t * tk, tk)
        dots = _dot(ki_ref[pl.ds(off, tk), :], rqi_ref[...])
        sc = jnp.zeros((tk, tq), F32)
        for h in range(IDX_HEADS):
            sc = sc + jnp.maximum(dots[:, h * tq:(h + 1) * tq], 0.0) * w_ref[h:h + 1, :]
        sc = jnp.where(krow + off <= qpos, sc, -jnp.inf)
        b = pltpu.bitcast(sc, I32)
        key = b ^ ((b >> 31) & 0x7FFFFFFF)
        key = jnp.where(key == -1, 0, key)
        hi_ref[pl.ds(off, tk), :] = (key >> 16).astype(I16)
        lo_ref[pl.ds(off, tk), :] = (key ^ 0x8000).astype(I16)
        return carry

    lax.fori_loop(0, n_t, score_body, 0)

    n_acc = 4
    one, zero = jnp.ones((PACKED, tq), BF16), jnp.zeros((PACKED, tq), BF16)

    n_main = n_t // MAIN_TILES
    n_tail = n_t - n_main * MAIN_TILES

    def count(ref, pred):
        def body(width, base, t, accs):
            accs = list(accs)
            tile = ref[pl.ds(pl.multiple_of(base + t * width, tk), width), :]
            for r in range(width // PACKED):
                x = tile[r * PACKED:(r + 1) * PACKED]
                accs[r % n_acc] = accs[r % n_acc] + jnp.where(pred(x), one, zero)
            return tuple(accs)
        accs = lax.fori_loop(0, n_main, functools.partial(body, MAIN_TILES * tk, 0), (zero,) * n_acc)
        accs = lax.fori_loop(0, n_tail, functools.partial(body, tk, n_main * MAIN_TILES * tk), accs)
        total = accs[0].astype(F32)
        for a in accs[1:]:
            total = total + a.astype(F32)
        return jnp.sum(total, axis=0, keepdims=True)

    def largest16(ref, want):
        c = count(ref, lambda x: x >= jnp.int16(0))
        ok = c >= want
        v = jnp.broadcast_to(jnp.where(ok, 0, -32768), (PACKED, tq)).astype(I32)
        above = jnp.where(ok, 0.0, c)

        def body(p, carry):
            v, above = carry
            cand = v | jnp.left_shift(jnp.int32(1), 14 - p)
            c16 = cand.astype(I16)
            c = count(ref, lambda x: x >= c16)
            ok = c >= want
            return jnp.where(ok, cand, v), jnp.where(ok, above, c)

        return lax.fori_loop(0, 15, body, (v, above))

    kf = np.float32(topk)
    thr_hi, above_hi = largest16(hi_ref, kf)
    thr_hi16 = thr_hi.astype(I16)
    thr_hi_tile = jnp.broadcast_to(thr_hi16[0:1], (tk, tq))

    def low_body(t, carry):
        rows = pl.ds(pl.multiple_of(t * tk, tk), tk)
        lo_ref[rows, :] = jnp.where(hi_ref[rows, :] == thr_hi_tile, lo_ref[rows, :], jnp.int16(-32768))
        return carry

    lax.fori_loop(0, n_t, low_body, 0)
    thr_lo, above_lo = largest16(lo_ref, kf - above_hi)
    thr_lo16 = thr_lo.astype(I16)
    thr = (thr_hi[0:1] << 16) | ((thr_lo[0:1] + 32768) & 0xFFFF)

    need = kf - above_hi - above_lo
    none = jnp.int16(32767)
    dead = jnp.broadcast_to(jnp.where(thr == NEG_INF_KEY, 32767, 0), (tk, tq)).astype(I16)
    thr_lo_tile = jnp.broadcast_to(thr_lo16[0:1], (tk, tq))

    def tie_body(t, carry):
        off = pl.multiple_of(t * tk, tk)
        rows = pl.ds(off, tk)
        pos = (krow + off).astype(I16) | dead
        at_lo = jnp.where(lo_ref[rows, :] == thr_lo_tile, pos, none)
        tie_ref[rows, :] = jnp.where(hi_ref[rows, :] == thr_hi_tile, at_lo, none)
        return carry

    lax.fori_loop(0, n_t, tie_body, 0)

    pos_bits = 32 - lax.clz(n_t * tk - 1)

    def cut_body(p, cut):
        cand = cut | jnp.left_shift(jnp.int32(1), pos_bits - 1 - p)
        c16 = cand.astype(I16)
        return jnp.where(count(tie_ref, lambda x: x < c16) < need, cand, cut)

    cut = lax.fori_loop(0, pos_bits, cut_body, jnp.zeros((PACKED, tq), I32))
    cut_row = jnp.where(thr != NEG_INF_KEY, cut[0:1], -1)

    acc_ref[...] = jnp.zeros(acc_ref.shape, F32)
    bound2 = jnp.max(q2, axis=1, keepdims=True) * kmax_ref[:, 0:1]
    safe = jnp.max(bound2) <= LOGIT_LIMIT * LOGIT_LIMIT

    one_t, zero_t = jnp.ones((tk, tq), BF16), jnp.zeros((tk, tq), BF16)
    cut_tile = jnp.broadcast_to(cut_row, (tk, tq)).astype(I16)

    def keep_of(rows):
        tied = jnp.where(tie_ref[rows, :] <= cut_tile, one_t, zero_t)
        return jnp.where(hi_ref[rows, :] > thr_hi_tile, one_t,
                         jnp.where(lo_ref[rows, :] > thr_lo_tile, one_t, tied))

    @pl.when(safe)
    def _():
        def body(t, carry):
            rows = pl.ds(pl.multiple_of(t * tk, tk), tk)
            keep = keep_of(rows)
            s_all = _dot(kv_ref[rows, :], rq_ref[...])
            probs = [jnp.exp(s_all[:, h * tq:(h + 1) * tq]).astype(BF16) * keep for h in range(N_HEADS_A)]
            acc_ref[...] += _dot(kvt_ref[:, rows], jnp.concatenate(probs, axis=1))
            return carry

        lax.fori_loop(0, n_t, body, 0)

    @pl.when(jnp.logical_not(safe))
    def _():
        m_ref[...] = jnp.full(m_ref.shape, MASKED, F32)

        def body(t, carry):
            off = pl.multiple_of(t * tk, tk)
            bias = (keep_of(pl.ds(off, tk)).astype(F32) - 1.0) * -MASKED
            s_all = _dot(kv_ref[pl.ds(off, tk), :], rq_ref[...])
            probs, alphas = [], []
            for h in range(N_HEADS_A):
                s = s_all[:, h * tq:(h + 1) * tq] + bias
                m_old = m_ref[h]
                m_new = jnp.maximum(m_old, jnp.max(s, axis=0, keepdims=True))
                probs.append(jnp.exp(s - m_new).astype(BF16))
                alphas.append(jnp.exp(m_old - m_new))
                m_ref[h] = m_new
            pv = _dot(kvt_ref[:, pl.ds(off, tk)], jnp.concatenate(probs, axis=1))
            for h in range(N_HEADS_A):
                cols = pl.ds(h * tq, tq)
                acc_ref[:, cols] = alphas[h] * acc_ref[:, cols] + pv[:, h * tq:(h + 1) * tq]
            return carry

        lax.fori_loop(0, n_t, body, 0)

    for pair in range(N_HEADS_A // 2):
        outs = []
        for h in (2 * pair, 2 * pair + 1):
            a = acc_ref[:, pl.ds(h * tq, tq)]
            outs.append(a[HEAD_DIM:] / a[0:1])
        o_ref[:, pair * LANES:(pair + 1) * LANES] = jnp.concatenate(outs, axis=0).T.astype(o_ref.dtype)


def _dsa_attention(proj, wi, *, tq=DSA_QUERY_BLOCK, tk=DSA_KEY_TILE):
    b, s, _ = proj.shape
    topk = min(TOPK_MAX, s // 4)
    assert s < 2 ** 15 and s % tk == 0 and tk % tq == 0
    qa_w = N_HEADS_A * HEAD_DIM
    qi_w = IDX_HEADS * IDX_DIM
    resident = dict(pipeline_mode=pl.Buffered(1))
    return pl.pallas_call(
        functools.partial(_dsa_kernel, tq=tq, tk=tk, topk=topk, seq=s),
        grid=(b, s // tq),
        in_specs=[
            pl.BlockSpec((None, tq, qa_w), lambda bi, i: (bi, i, COL_QA // qa_w)),
            pl.BlockSpec((None, tq, qi_w), lambda bi, i: (bi, i, COL_QI // qi_w)),
            pl.BlockSpec((None, s, LANES), lambda bi, i: (bi, 0, COL_KVA // LANES), **resident),
            pl.BlockSpec((None, s, LANES), lambda bi, i: (bi, 0, COL_KI // LANES), **resident),
            pl.BlockSpec((None, tq, LANES), lambda bi, i: (bi, i, 0)),
        ],
        out_specs=pl.BlockSpec((None, tq, qa_w), lambda bi, i: (bi, i, 0)),
        out_shape=jax.ShapeDtypeStruct((b, s, qa_w), BF16),
        scratch_shapes=[
            pltpu.VMEM((s, tq), I16),
            pltpu.VMEM((s, tq), I16),
            pltpu.VMEM((s, tq), I16),
            pltpu.VMEM((LANES, s), BF16),
            pltpu.VMEM((LANES, N_HEADS_A * tq), BF16),
            pltpu.VMEM((LANES, IDX_HEADS * tq), BF16),
            pltpu.VMEM((SUBLANES, tq), F32),
            pltpu.VMEM((N_HEADS_A, 1, tq), F32),
            pltpu.VMEM((LANES, N_HEADS_A * tq), F32),
            pltpu.VMEM((1, LANES), F32),
        ],
        compiler_params=_params("parallel", "arbitrary", vmem=DSA_VMEM_LIMIT),
        name="dsa_attention",
    )(proj, proj, proj, proj, wi)


def _sb_kernel(q_ref, k_ref, v_ref, u_ref, o_ref, c_ref, acc_ref, *, tq, tk):
    i = pl.program_id(1)
    q0 = i * tq
    n_t = (q0 + tq - 1) // tk + 1
    n_pairs = N_HEADS_B // 2
    lane = lax.broadcasted_iota(I32, (tq, LANES), 1)
    low_half = lane < HEAD_DIM
    zero = jnp.zeros((tq, LANES), BF16)
    qh = []
    for p in range(n_pairs):
        qp = q_ref[:, p * LANES:(p + 1) * LANES]
        qh += [jnp.where(low_half, qp, zero), jnp.where(low_half, zero, qp)]
    row = lax.broadcasted_iota(I32, (tq, tk), 0) + q0
    col0 = lax.broadcasted_iota(I32, (tq, tk), 1)
    upper = u_ref[...]

    c_ref[...] = jnp.zeros(c_ref.shape, F32)
    acc_ref[...] = jnp.zeros(acc_ref.shape, F32)

    def tile(t, masked):
        off = pl.multiple_of(t * tk, tk)
        strict = (col0 + off) < row
        ks = [k_ref[pl.ds(off, tk), p * LANES:(p + 1) * LANES] for p in range(n_pairs)]
        vs = [v_ref[pl.ds(off, tk), p * LANES:(p + 1) * LANES] for p in range(n_pairs)]
        zs = [_dot_nt(qh[h], ks[h // 2]) for h in range(N_HEADS_B)]
        log_hits, fails, sums = [], [], []
        for z in zs:
            neg_abs = pltpu.bitcast(pltpu.bitcast(z, jnp.uint32) | jnp.uint32(0x80000000), F32)
            soft = jnp.log2(1.0 + jnp.exp2(neg_abs))
            log_hit = jnp.minimum(z, 0.0) - soft
            log_fail = log_hit - z
            if masked:
                log_fail = jnp.where(strict, log_fail, 0.0)
            log_hits.append(log_hit)
            fails.append(log_fail.astype(BF16))
            sums.append(jnp.sum(log_fail, axis=1, keepdims=True))
        afters = [_dot(f, upper) for f in fails]
        probs = []
        for h in range(N_HEADS_B):
            a = jnp.exp2(log_hits[h] + (afters[h] + c_ref[h]))
            if masked:
                a = jnp.where(strict, a, 0.0)
            probs.append(a.astype(BF16))
        pvs = [_dot(probs[h], vs[h // 2]) for h in range(N_HEADS_B)]
        for p in range(n_pairs):
            acc_ref[p] += jnp.where(low_half, pvs[2 * p], pvs[2 * p + 1])
        for h in range(N_HEADS_B):
            c_ref[h] += sums[h]

    tile(n_t - 1, True)

    def alive():
        return jnp.max(c_ref[...]) > DEAD_LOG2

    def cond(state):
        j, live = state
        return jnp.logical_and(j < n_t - 1, live)

    def body(state):
        j, _ = state
        tile(n_t - 2 - j, False)
        return j + 1, alive()

    lax.while_loop(cond, body, (jnp.int32(0), alive()))
    for p in range(n_pairs):
        o_ref[:, p * LANES:(p + 1) * LANES] = acc_ref[p].astype(o_ref.dtype)


def _sb_attention(proj, *, tq=SB_QUERY_BLOCK, tk=SB_KEY_TILE):
    b, s, _ = proj.shape
    assert s % tk == 0 and s % tq == 0
    w = N_HEADS_B * HEAD_DIM
    upper = (np.arange(tk)[:, None] > np.arange(tk)[None, :]).astype(np.float32)
    resident = dict(pipeline_mode=pl.Buffered(1))
    return pl.pallas_call(
        functools.partial(_sb_kernel, tq=tq, tk=tk),
        grid=(b, s // tq),
        in_specs=[
            pl.BlockSpec((None, tq, w), lambda bi, i: (bi, i, COL_QB // w)),
            pl.BlockSpec((None, s, w), lambda bi, i: (bi, 0, COL_KB // w), **resident),
            pl.BlockSpec((None, s, w), lambda bi, i: (bi, 0, COL_VB // w), **resident),
            pl.BlockSpec((tk, tk), lambda bi, i: (0, 0)),
        ],
        out_specs=pl.BlockSpec((None, tq, w), lambda bi, i: (bi, i, 0)),
        out_shape=jax.ShapeDtypeStruct((b, s, w), BF16),
        scratch_shapes=[
            pltpu.VMEM((N_HEADS_B, tq, 1), F32),
            pltpu.VMEM((N_HEADS_B // 2, tq, LANES), F32),
        ],
        compiler_params=_params("parallel", "arbitrary"),
        name="stick_breaking_attention",
    )(proj, proj, proj, jnp.asarray(upper, BF16))


def _gmlp_tail_kernel(u_ref, v_ref, gv_ref, bv_ref, ws_ref, bs_ref, wo_ref, g_ref, h_ref, o_ref,
                      wt_ref, vn_ref, gated_ref, *, tm):
    @pl.when(pl.program_id(0) == 0)
    def _():
        r = lax.broadcasted_iota(I32, (CHUNK, CHUNK), 0)
        c = lax.broadcasted_iota(I32, (CHUNK, CHUNK), 1)
        for g in range(C_GROUPS):
            wt_ref[g] = jnp.where(c <= r, ws_ref[g], 0.0).astype(BF16)

    v = v_ref[...].astype(F32)
    mu = jnp.mean(v, axis=-1, keepdims=True)
    var = jnp.mean(jnp.square(v - mu), axis=-1, keepdims=True)
    vn_ref[...] = (((v - mu) * lax.rsqrt(var + EPS)) * gv_ref[...] + bv_ref[...]).astype(BF16)
    cw = C_WIDTH // C_GROUPS
    chunks = [slice(ch * CHUNK, (ch + 1) * CHUNK) for ch in range(tm // CHUNK)]
    for g in range(C_GROUPS):
        cols = slice(g * cw, (g + 1) * cw)
        sv = _dot(wt_ref[g], jnp.concatenate([vn_ref[rows, cols] for rows in chunks], axis=1))
        for ch, rows in enumerate(chunks):
            mixed = sv[:, ch * cw:(ch + 1) * cw] + bs_ref[g]
            gated_ref[rows, cols] = (u_ref[rows, cols].astype(F32) * mixed).astype(BF16)
    y = _dot(gated_ref[...], wo_ref[...])
    o_ref[...] = h_ref[...] + _rms(y, g_ref[...])


def _gmlp_tail(z, gv, bv, ws, bs_b, wo, g, h, *, tm=ROW_BLOCK):
    t, d = h.shape
    cw = C_WIDTH // C_GROUPS
    return pl.pallas_call(
        functools.partial(_gmlp_tail_kernel, tm=tm),
        grid=(t // tm,),
        in_specs=[
            pl.BlockSpec((tm, C_WIDTH), lambda i: (i, 0)),
            pl.BlockSpec((tm, C_WIDTH), lambda i: (i, 1)),
            pl.BlockSpec((1, C_WIDTH), lambda i: (0, 0)),
            pl.BlockSpec((1, C_WIDTH), lambda i: (0, 0)),
            pl.BlockSpec((C_GROUPS, CHUNK, CHUNK), lambda i: (0, 0, 0)),
            pl.BlockSpec((C_GROUPS, CHUNK, cw), lambda i: (0, 0, 0)),
            pl.BlockSpec((C_WIDTH, d), lambda i: (0, 0)),
            pl.BlockSpec((1, d), lambda i: (0, 0)),
            pl.BlockSpec((tm, d), lambda i: (i, 0)),
        ],
        out_specs=pl.BlockSpec((tm, d), lambda i: (i, 0)),
        out_shape=jax.ShapeDtypeStruct((t, d), F32),
        scratch_shapes=[
            pltpu.VMEM((C_GROUPS, CHUNK, CHUNK), BF16),
            pltpu.VMEM((tm, C_WIDTH), BF16),
            pltpu.VMEM((tm, C_WIDTH), BF16),
        ],
        compiler_params=_params("arbitrary"),
        name="gmlp_tail",
    )(z, z, gv, bv, ws, bs_b, wo, g, h)


def _prep_even_weights(w):
    split_points = np.cumsum(EVEN_WIDTHS)[:-1].tolist()
    qa, ka, va, qi, ki, wi, qb, kb, vb = jnp.split(w, split_points, axis=1)
    d = w.shape[0]
    main = jnp.concatenate(
        [qa * (HEAD_DIM ** -0.5), qi * (IDX_DIM ** -0.5), ka, va, ki, jnp.zeros((d, LANES - IDX_DIM), w.dtype),
         qb * (HEAD_DIM ** -0.5 * LOG2_E), kb, vb], axis=1).astype(BF16)
    wi_cols = jnp.concatenate(
        [wi * (IDX_HEADS ** -0.5), jnp.zeros((d, LANES - IDX_HEADS), w.dtype)], axis=1).astype(BF16)
    return main, wi_cols


def kernel(x, norm_gains, w_in_even, w_out_even, w_in_odd, spatial_w, spatial_b, v_norm_gain, v_norm_bias,
           w_out_odd, w_ffn_in, w_ffn_out):
    b, s, d = x.shape
    t = b * s
    depth = norm_gains.shape[0]
    h = x.reshape(t, d)
    tm = ROW_BLOCK
    mix_a = N_HEADS_A * HEAD_DIM
    for layer in range(depth):
        g = norm_gains[layer][:, None, :]
        i = layer // 2
        if layer % 2 == 0:
            w_main, w_wi = _prep_even_weights(w_in_even[i])
            proj, wi = _norm_matmul(h, g[0], w_main, act="none", tm=tm, tn=EVEN_COLS // 5, side_w=w_wi)
            proj = proj.reshape(b, s, EVEN_COLS)
            ya = _dsa_attention(proj, wi.reshape(b, s, LANES))
            yb = _sb_attention(proj)
            wo = w_out_even[i].astype(BF16)
            h = _mm_norm_res([ya.reshape(t, mix_a), yb.reshape(t, -1)], [wo[:mix_a], wo[mix_a:]], g[1], h, tm=tm)
        else:
            z = _norm_matmul(h, g[0], w_in_odd[i].astype(BF16), act="gelu", tm=tm, tn=1024)
            bs_b = jnp.broadcast_to(spatial_b[i][:, :, None], (C_GROUPS, CHUNK, C_WIDTH // C_GROUPS))
            h = _gmlp_tail(z, v_norm_gain[i][None, :], v_norm_bias[i][None, :], spatial_w[i], bs_b,
                           w_out_odd[i].astype(BF16), g[1], h)
        h = _ffn(h, g[2], w_ffn_in[layer].astype(BF16), w_ffn_out[layer].astype(BF16), g[3],
                 tm=ROW_BLOCK, tn=D_FF // 11)
    return h.reshape(b, s, d)
```

```python
import functools

import numpy as np
import jax
import jax.numpy as jnp
from jax import lax
from jax.experimental import pallas as pl
from jax.experimental.pallas import tpu as pltpu

F32 = jnp.float32
BF16 = jnp.bfloat16
I32 = jnp.int32
I16 = jnp.int16

D_MODEL = 1024
HEAD_DIM = 64
N_HEADS_A = 8
N_HEADS_B = 8
IDX_HEADS = 4
IDX_DIM = 64
TOPK_MAX = 256
BLOCK_Q = 128
CHUNK = 128
C_WIDTH = 2048
C_GROUPS = 16
D_FF = 2816
EPS = 1e-6
EVEN_WIDTHS = (N_HEADS_A * HEAD_DIM, HEAD_DIM, HEAD_DIM, IDX_HEADS * IDX_DIM, IDX_DIM, IDX_HEADS,
               N_HEADS_B * HEAD_DIM, N_HEADS_B * HEAD_DIM, N_HEADS_B * HEAD_DIM)

LANES = 128
SUBLANES = 8
PACKED = 16
ROW_BLOCK = 512
DSA_QUERY_BLOCK = 512
DSA_KEY_TILE = 512
SB_QUERY_BLOCK = 256
SB_KEY_TILE = 256
MAIN_TILES = 4
LOGIT_LIMIT = 40.0
VMEM_LIMIT = 48 * 1024 * 1024
DSA_VMEM_LIMIT = 56 * 1024 * 1024

COL_QA = 0
COL_QI = 512
COL_KVA = 768
COL_KI = 896
COL_QB = 1024
COL_KB = 1536
COL_VB = 2048
EVEN_COLS = 2560

LOG2_E = float(np.log2(np.e))
DEAD_LOG2 = -160.0
MASKED = -1e30
NEG_INF_KEY = -2139095041


def _params(*sem, vmem=VMEM_LIMIT, fuse_inputs=None):
    return pltpu.CompilerParams(dimension_semantics=sem, vmem_limit_bytes=vmem, allow_input_fusion=fuse_inputs)


def _dot(a, b):
    return jnp.dot(a, b, preferred_element_type=F32)


def _dot_nt(a, b):
    return lax.dot_general(a, b, (((1,), (1,)), ((), ())), preferred_element_type=F32)


def _rms(y, g):
    ms = jnp.mean(y * y, axis=-1, keepdims=True)
    return (y * lax.rsqrt(ms + EPS)) * g


def _gelu_tanh(x):
    c = np.float32(np.sqrt(2.0 / np.pi))
    return x * (0.5 * (1.0 + jnp.tanh(c * (x + 0.044715 * (x * x * x)))))


def _norm_matmul_kernel(x_ref, g_ref, w_ref, *refs, act, tn, n_side):
    o_ref = refs[n_side]
    xn = _rms(x_ref[...], g_ref[...]).astype(BF16)
    n_out = o_ref.shape[1]
    for j in range(n_out // tn):
        y = _dot(xn, w_ref[:, j * tn:(j + 1) * tn])
        if act == "gelu":
            y = _gelu_tanh(y)
        o_ref[:, j * tn:(j + 1) * tn] = y.astype(o_ref.dtype)
    if n_side:
        refs[n_side + 1][...] = _dot(xn, refs[0][...])


def _norm_matmul(x, g, w, *, act, tm, tn, side_w=None):
    t, d = x.shape
    n_out = w.shape[1]
    n_side = 0 if side_w is None else 1
    resident = dict(pipeline_mode=pl.Buffered(1))
    in_specs = [pl.BlockSpec((tm, d), lambda i: (i, 0)),
                pl.BlockSpec((1, d), lambda i: (0, 0)),
                pl.BlockSpec(w.shape, lambda i: (0, 0), **resident)]
    out_specs = [pl.BlockSpec((tm, n_out), lambda i: (i, 0))]
    out_shape = [jax.ShapeDtypeStruct((t, n_out), BF16)]
    args = [x, g, w]
    if n_side:
        in_specs.append(pl.BlockSpec(side_w.shape, lambda i: (0, 0), **resident))
        out_specs.append(pl.BlockSpec((tm, side_w.shape[1]), lambda i: (i, 0)))
        out_shape.append(jax.ShapeDtypeStruct((t, side_w.shape[1]), F32))
        args.append(side_w)
    outs = pl.pallas_call(
        functools.partial(_norm_matmul_kernel, act=act, tn=tn, n_side=n_side),
        grid=(t // tm,),
        in_specs=in_specs,
        out_specs=out_specs,
        out_shape=out_shape,
        compiler_params=_params("parallel"),
        name="norm_matmul_" + act,
    )(*args)
    return outs if n_side else outs[0]


def _mm_norm_res_kernel(*refs, n_in):
    x_refs, w_refs = refs[:n_in], refs[n_in:2 * n_in]
    g_ref, h_ref, o_ref = refs[2 * n_in:]
    y = _dot(x_refs[0][...], w_refs[0][...])
    for k in range(1, n_in):
        y = y + _dot(x_refs[k][...], w_refs[k][...])
    o_ref[...] = h_ref[...] + _rms(y, g_ref[...])


def _mm_norm_res(xs, ws, g, h, *, tm):
    t, d = h.shape
    n_in = len(xs)
    in_specs = [pl.BlockSpec((tm, x.shape[1]), lambda i: (i, 0)) for x in xs]
    in_specs += [pl.BlockSpec(w.shape, lambda i: (0, 0)) for w in ws]
    in_specs += [pl.BlockSpec((1, d), lambda i: (0, 0)), pl.BlockSpec((tm, d), lambda i: (i, 0))]
    return pl.pallas_call(
        functools.partial(_mm_norm_res_kernel, n_in=n_in),
        grid=(t // tm,),
        in_specs=in_specs,
        out_specs=pl.BlockSpec((tm, d), lambda i: (i, 0)),
        out_shape=jax.ShapeDtypeStruct((t, d), F32),
        compiler_params=_params("parallel"),
        name="matmul_norm_residual",
    )(*xs, *ws, g, h)


def _ffn_kernel(h_ref, g_in_ref, w_in_ref, w_out_ref, g_out_ref, o_ref, *, tn):
    h = h_ref[...]
    xn = _rms(h, g_in_ref[...]).astype(BF16)
    d_ff = w_out_ref.shape[0]
    y = None
    for j in range(d_ff // tn):
        gate = _dot(xn, w_in_ref[:, j * tn:(j + 1) * tn])
        up = _dot(xn, w_in_ref[:, d_ff + j * tn:d_ff + (j + 1) * tn])
        act = ((gate / (1.0 + jnp.exp(-gate))) * up).astype(BF16)
        part = _dot(act, w_out_ref[j * tn:(j + 1) * tn, :])
        y = part if y is None else y + part
    o_ref[...] = h + _rms(y, g_out_ref[...])


def _ffn(h, g_in, w_in, w_out, g_out, *, tm, tn):
    t, d = h.shape
    resident = dict(pipeline_mode=pl.Buffered(1))
    return pl.pallas_call(
        functools.partial(_ffn_kernel, tn=tn),
        grid=(t // tm,),
        in_specs=[pl.BlockSpec((tm, d), lambda i: (i, 0)),
                  pl.BlockSpec((1, d), lambda i: (0, 0)),
                  pl.BlockSpec(w_in.shape, lambda i: (0, 0), **resident),
                  pl.BlockSpec(w_out.shape, lambda i: (0, 0), **resident),
                  pl.BlockSpec((1, d), lambda i: (0, 0))],
        out_specs=pl.BlockSpec((tm, d), lambda i: (i, 0)),
        out_shape=jax.ShapeDtypeStruct((t, d), F32),
        compiler_params=_params("parallel", fuse_inputs=[False, False, True, True, False]),
        name="ffn_fused",
    )(h, g_in, w_in, w_out, g_out)


def _dsa_kernel(q_ref, qi_ref, kv_ref, ki_ref, wi_ref, o_ref,
                hi_ref, lo_ref, tie_ref, kvt_ref, rq_ref, rqi_ref, w_ref, m_ref, acc_ref, kmax_ref,
                *, tq, tk, topk, seq):
    i = pl.program_id(1)
    q0 = i * tq
    n_t = q0 // tk + 1
    half_rows = lax.broadcasted_iota(I32, (LANES, tq), 0) < HEAD_DIM

    @pl.when(i == 0)
    def _():
        feat_is_k = lax.broadcasted_iota(I32, (LANES, LANES), 0) < HEAD_DIM

        def blk(j, kmax2):
            off = pl.multiple_of(j * LANES, LANES)
            t = kv_ref[pl.ds(off, LANES), :].astype(F32).T
            kvt_ref[:, pl.ds(off, LANES)] = jnp.where(feat_is_k, 1.0, t).astype(BF16)
            return jnp.maximum(kmax2, jnp.sum(jnp.where(feat_is_k, t * t, 0.0), axis=0, keepdims=True))

        kmax2 = lax.fori_loop(0, seq // LANES, blk, jnp.zeros((1, LANES), F32))
        kmax_ref[...] = jnp.broadcast_to(jnp.max(kmax2, axis=1, keepdims=True), (1, LANES))

    zeros_half = jnp.zeros((HEAD_DIM, tq), F32)
    q2 = jnp.zeros((1, tq), F32)
    for pair in range(N_HEADS_A // 2):
        t = q_ref[:, pair * LANES:(pair + 1) * LANES].astype(F32).T
        even = jnp.where(half_rows, t, 0.0)
        odd = jnp.concatenate([t[HEAD_DIM:], zeros_half], axis=0)
        rq_ref[:, pl.ds(2 * pair * tq, tq)] = even.astype(BF16)
        rq_ref[:, pl.ds((2 * pair + 1) * tq, tq)] = odd.astype(BF16)
        q2 = jnp.maximum(q2, jnp.sum(t * t, axis=0, keepdims=True))
    for pair in range(IDX_HEADS // 2):
        t = qi_ref[:, pair * LANES:(pair + 1) * LANES].astype(F32).T
        swapped = jnp.concatenate([t[HEAD_DIM:], t[:HEAD_DIM]], axis=0)
        rqi_ref[:, pl.ds(2 * pair * tq, tq)] = t.astype(BF16)
        rqi_ref[:, pl.ds((2 * pair + 1) * tq, tq)] = swapped.astype(BF16)
    w_ref[...] = wi_ref[...].T[:SUBLANES]

    qpos = lax.broadcasted_iota(I32, (tk, tq), 1) + q0
    krow = lax.broadcasted_iota(I32, (tk, tq), 0)

    def score_body(t, carry):
        off = pl.multiple_of(t * tk, tk)
        dots = _dot(ki_ref[pl.ds(off, tk), :], rqi_ref[...])
        sc = jnp.zeros((tk, tq), F32)
        for h in range(IDX_HEADS):
            sc = sc + jnp.maximum(dots[:, h * tq:(h + 1) * tq], 0.0) * w_ref[h:h + 1, :]
        sc = jnp.where(krow + off <= qpos, sc, -jnp.inf)
        b = pltpu.bitcast(sc, I32)
        key = b ^ ((b >> 31) & 0x7FFFFFFF)
        key = jnp.where(key == -1, 0, key)
        hi_ref[pl.ds(off, tk), :] = (key >> 16).astype(I16)
        lo_ref[pl.ds(off, tk), :] = (key ^ 0x8000).astype(I16)
        return carry

    lax.fori_loop(0, n_t, score_body, 0)

    n_acc = 4
    one, zero = jnp.ones((PACKED, tq), BF16), jnp.zeros((PACKED, tq), BF16)

    n_main = n_t // MAIN_TILES
    n_tail = n_t - n_main * MAIN_TILES

    def count(ref, pred):
        def body(width, base, t, accs):
            accs = list(accs)
            tile = ref[pl.ds(pl.multiple_of(base + t * width, tk), width), :]
            for r in range(width // PACKED):
                x = tile[r * PACKED:(r + 1) * PACKED]
                accs[r % n_acc] = accs[r % n_acc] + jnp.where(pred(x), one, zero)
            return tuple(accs)
        accs = lax.fori_loop(0, n_main, functools.partial(body, MAIN_TILES * tk, 0), (zero,) * n_acc)
        accs = lax.fori_loop(0, n_tail, functools.partial(body, tk, n_main * MAIN_TILES * tk), accs)
        total = accs[0].astype(F32)
        for a in accs[1:]:
            total = total + a.astype(F32)
        return jnp.sum(total, axis=0, keepdims=True)

    def largest16(ref, want):
        c = count(ref, lambda x: x >= jnp.int16(0))
        ok = c >= want
        v = jnp.broadcast_to(jnp.where(ok, 0, -32768), (PACKED, tq)).astype(I32)
        above = jnp.where(ok, 0.0, c)

        def body(p, carry):
            v, above = carry
            cand = v | jnp.left_shift(jnp.int32(1), 14 - p)
            c16 = cand.astype(I16)
            c = count(ref, lambda x: x >= c16)
            ok = c >= want
            return jnp.where(ok, cand, v), jnp.where(ok, above, c)

        return lax.fori_loop(0, 15, body, (v, above))

    kf = np.float32(topk)
    thr_hi, above_hi = largest16(hi_ref, kf)
    thr_hi16 = thr_hi.astype(I16)
    thr_hi_tile = jnp.broadcast_to(thr_hi16[0:1], (tk, tq))

    def low_body(t, carry):
        rows = pl.ds(pl.multiple_of(t * tk, tk), tk)
        lo_ref[rows, :] = jnp.where(hi_ref[rows, :] == thr_hi_tile, lo_ref[rows, :], jnp.int16(-32768))
        return carry

    lax.fori_loop(0, n_t, low_body, 0)
    thr_lo, above_lo = largest16(lo_ref, kf - above_hi)
    thr_lo16 = thr_lo.astype(I16)
    thr = (thr_hi[0:1] << 16) | ((thr_lo[0:1] + 32768) & 0xFFFF)

    need = kf - above_hi - above_lo
    none = jnp.int16(32767)
    dead = jnp.broadcast_to(jnp.where(thr == NEG_INF_KEY, 32767, 0), (tk, tq)).astype(I16)
    thr_lo_tile = jnp.broadcast_to(thr_lo16[0:1], (tk, tq))

    def tie_body(t, carry):
        off = pl.multiple_of(t * tk, tk)
        rows = pl.ds(off, tk)
        pos = (krow + off).astype(I16) | dead
        at_lo = jnp.where(lo_ref[rows, :] == thr_lo_tile, pos, none)
        tie_ref[rows, :] = jnp.where(hi_ref[rows, :] == thr_hi_tile, at_lo, none)
        return carry

    lax.fori_loop(0, n_t, tie_body, 0)

    pos_bits = 32 - lax.clz(n_t * tk - 1)

    def cut_body(p, cut):
        cand = cut | jnp.left_shift(jnp.int32(1), pos_bits - 1 - p)
        c16 = cand.astype(I16)
        return jnp.where(count(tie_ref, lambda x: x < c16) < need, cand, cut)

    cut = lax.fori_loop(0, pos_bits, cut_body, jnp.zeros((PACKED, tq), I32))
    cut_row = jnp.where(thr != NEG_INF_KEY, cut[0:1], -1)

    acc_ref[...] = jnp.zeros(acc_ref.shape, F32)
    bound2 = jnp.max(q2, axis=1, keepdims=True) * kmax_ref[:, 0:1]
    safe = jnp.max(bound2) <= LOGIT_LIMIT * LOGIT_LIMIT

    one_t, zero_t = jnp.ones((tk, tq), BF16), jnp.zeros((tk, tq), BF16)
    cut_tile = jnp.broadcast_to(cut_row, (tk, tq)).astype(I16)

    def keep_of(rows):
        tied = jnp.where(tie_ref[rows, :] <= cut_tile, one_t, zero_t)
        return jnp.where(hi_ref[rows, :] > thr_hi_tile, one_t,
                         jnp.where(lo_ref[rows, :] > thr_lo_tile, one_t, tied))

    @pl.when(safe)
    def _():
        def body(t, carry):
            rows = pl.ds(pl.multiple_of(t * tk, tk), tk)
            keep = keep_of(rows)
            s_all = _dot(kv_ref[rows, :], rq_ref[...])
            probs = [jnp.exp(s_all[:, h * tq:(h + 1) * tq]).astype(BF16) * keep for h in range(N_HEADS_A)]
            acc_ref[...] += _dot(kvt_ref[:, rows], jnp.concatenate(probs, axis=1))
            return carry

        lax.fori_loop(0, n_t, body, 0)

    @pl.when(jnp.logical_not(safe))
    def _():
        m_ref[...] = jnp.full(m_ref.shape, MASKED, F32)

        def body(t, carry):
            off = pl.multiple_of(t * tk, tk)
            bias = (keep_of(pl.ds(off, tk)).astype(F32) - 1.0) * -MASKED
            s_all = _dot(kv_ref[pl.ds(off, tk), :], rq_ref[...])
            probs, alphas = [], []
            for h in range(N_HEADS_A):
                s = s_all[:, h * tq:(h + 1) * tq] + bias
                m_old = m_ref[h]
                m_new = jnp.maximum(m_old, jnp.max(s, axis=0, keepdims=True))
                probs.append(jnp.exp(s - m_new).astype(BF16))
                alphas.append(jnp.exp(m_old - m_new))
                m_ref[h] = m_new
            pv = _dot(kvt_ref[:, pl.ds(off, tk)], jnp.concatenate(probs, axis=1))
            for h in range(N_HEADS_A):
                cols = pl.ds(h * tq, tq)
                acc_ref[:, cols] = alphas[h] * acc_ref[:, cols] + pv[:, h * tq:(h + 1) * tq]
            return carry

        lax.fori_loop(0, n_t, body, 0)

    for pair in range(N_HEADS_A // 2):
        outs = []
        for h in (2 * pair, 2 * pair + 1):
            a = acc_ref[:, pl.ds(h * tq, tq)]
            outs.append(a[HEAD_DIM:] / a[0:1])
        o_ref[:, pair * LANES:(pair + 1) * LANES] = jnp.concatenate(outs, axis=0).T.astype(o_ref.dtype)


def _dsa_attention(proj, wi, *, tq=DSA_QUERY_BLOCK, tk=DSA_KEY_TILE):
    b, s, _ = proj.shape
    topk = min(TOPK_MAX, s // 4)
    assert s < 2 ** 15 and s % tk == 0 and tk % tq == 0
    qa_w = N_HEADS_A * HEAD_DIM
    qi_w = IDX_HEADS * IDX_DIM
    resident = dict(pipeline_mode=pl.Buffered(1))
    return pl.pallas_call(
        functools.partial(_dsa_kernel, tq=tq, tk=tk, topk=topk, seq=s),
        grid=(b, s // tq),
        in_specs=[
            pl.BlockSpec((None, tq, qa_w), lambda bi, i: (bi, i, COL_QA // qa_w)),
            pl.BlockSpec((None, tq, qi_w), lambda bi, i: (bi, i, COL_QI // qi_w)),
            pl.BlockSpec((None, s, LANES), lambda bi, i: (bi, 0, COL_KVA // LANES), **resident),
            pl.BlockSpec((None, s, LANES), lambda bi, i: (bi, 0, COL_KI // LANES), **resident),
            pl.BlockSpec((None, tq, LANES), lambda bi, i: (bi, i, 0)),
        ],
        out_specs=pl.BlockSpec((None, tq, qa_w), lambda bi, i: (bi, i, 0)),
        out_shape=jax.ShapeDtypeStruct((b, s, qa_w), BF16),
        scratch_shapes=[
            pltpu.VMEM((s, tq), I16),
            pltpu.VMEM((s, tq), I16),
            pltpu.VMEM((s, tq), I16),
            pltpu.VMEM((LANES, s), BF16),
            pltpu.VMEM((LANES, N_HEADS_A * tq), BF16),
            pltpu.VMEM((LANES, IDX_HEADS * tq), BF16),
            pltpu.VMEM((SUBLANES, tq), F32),
            pltpu.VMEM((N_HEADS_A, 1, tq), F32),
            pltpu.VMEM((LANES, N_HEADS_A * tq), F32),
            pltpu.VMEM((1, LANES), F32),
        ],
        compiler_params=_params("parallel", "arbitrary", vmem=DSA_VMEM_LIMIT),
        name="dsa_attention",
    )(proj, proj, proj, proj, wi)


def _sb_kernel(q_ref, k_ref, v_ref, u_ref, o_ref, c_ref, acc_ref, *, tq, tk):
    i = pl.program_id(1)
    q0 = i * tq
    n_t = (q0 + tq - 1) // tk + 1
    n_pairs = N_HEADS_B // 2
    lane = lax.broadcasted_iota(I32, (tq, LANES), 1)
    low_half = lane < HEAD_DIM
    zero = jnp.zeros((tq, LANES), BF16)
    qh = []
    for p in range(n_pairs):
        qp = q_ref[:, p * LANES:(p + 1) * LANES]
        qh += [jnp.where(low_half, qp, zero), jnp.where(low_half, zero, qp)]
    row = lax.broadcasted_iota(I32, (tq, tk), 0) + q0
    col0 = lax.broadcasted_iota(I32, (tq, tk), 1)
    upper = u_ref[...]

    c_ref[...] = jnp.zeros(c_ref.shape, F32)
    acc_ref[...] = jnp.zeros(acc_ref.shape, F32)

    def tile(t, masked):
        off = pl.multiple_of(t * tk, tk)
        strict = (col0 + off) < row
        ks = [k_ref[pl.ds(off, tk), p * LANES:(p + 1) * LANES] for p in range(n_pairs)]
        vs = [v_ref[pl.ds(off, tk), p * LANES:(p + 1) * LANES] for p in range(n_pairs)]
        zs = [_dot_nt(qh[h], ks[h // 2]) for h in range(N_HEADS_B)]
        log_hits, fails, sums = [], [], []
        for z in zs:
            neg_abs = pltpu.bitcast(pltpu.bitcast(z, jnp.uint32) | jnp.uint32(0x80000000), F32)
            soft = jnp.log2(1.0 + jnp.exp2(neg_abs))
            log_hit = jnp.minimum(z, 0.0) - soft
            log_fail = log_hit - z
            if masked:
                log_fail = jnp.where(strict, log_fail, 0.0)
            log_hits.append(log_hit)
            fails.append(log_fail.astype(BF16))
            sums.append(jnp.sum(log_fail, axis=1, keepdims=True))
        afters = [_dot(f, upper) for f in fails]
        probs = []
        for h in range(N_HEADS_B):
            a = jnp.exp2(log_hits[h] + (afters[h] + c_ref[h]))
            if masked:
                a = jnp.where(strict, a, 0.0)
            probs.append(a.astype(BF16))
        pvs = [_dot(probs[h], vs[h // 2]) for h in range(N_HEADS_B)]
        for p in range(n_pairs):
            acc_ref[p] += jnp.where(low_half, pvs[2 * p], pvs[2 * p + 1])
        for h in range(N_HEADS_B):
            c_ref[h] += sums[h]

    tile(n_t - 1, True)

    def alive():
        return jnp.max(c_ref[...]) > DEAD_LOG2

    def cond(state):
        j, live = state
        return jnp.logical_and(j < n_t - 1, live)

    def body(state):
        j, _ = state
        tile(n_t - 2 - j, False)
        return j + 1, alive()

    lax.while_loop(cond, body, (jnp.int32(0), alive()))
    for p in range(n_pairs):
        o_ref[:, p * LANES:(p + 1) * LANES] = acc_ref[p].astype(o_ref.dtype)


def _sb_attention(proj, *, tq=SB_QUERY_BLOCK, tk=SB_KEY_TILE):
    b, s, _ = proj.shape
    assert s % tk == 0 and s % tq == 0
    w = N_HEADS_B * HEAD_DIM
    upper = (np.arange(tk)[:, None] > np.arange(tk)[None, :]).astype(np.float32)
    resident = dict(pipeline_mode=pl.Buffered(1))
    return pl.pallas_call(
        functools.partial(_sb_kernel, tq=tq, tk=tk),
        grid=(b, s // tq),
        in_specs=[
            pl.BlockSpec((None, tq, w), lambda bi, i: (bi, i, COL_QB // w)),
            pl.BlockSpec((None, s, w), lambda bi, i: (bi, 0, COL_KB // w), **resident),
            pl.BlockSpec((None, s, w), lambda bi, i: (bi, 0, COL_VB // w), **resident),
            pl.BlockSpec((tk, tk), lambda bi, i: (0, 0)),
        ],
        out_specs=pl.BlockSpec((None, tq, w), lambda bi, i: (bi, i, 0)),
        out_shape=jax.ShapeDtypeStruct((b, s, w), BF16),
        scratch_shapes=[
            pltpu.VMEM((N_HEADS_B, tq, 1), F32),
            pltpu.VMEM((N_HEADS_B // 2, tq, LANES), F32),
        ],
        compiler_params=_params("parallel", "arbitrary"),
        name="stick_breaking_attention",
    )(proj, proj, proj, jnp.asarray(upper, BF16))


def _gmlp_tail_kernel(u_ref, v_ref, gv_ref, bv_ref, ws_ref, bs_ref, wo_ref, g_ref, h_ref, o_ref,
                      wt_ref, vn_ref, gated_ref, *, tm):
    @pl.when(pl.program_id(0) == 0)
    def _():
        r = lax.broadcasted_iota(I32, (CHUNK, CHUNK), 0)
        c = lax.broadcasted_iota(I32, (CHUNK, CHUNK), 1)
        for g in range(C_GROUPS):
            wt_ref[g] = jnp.where(c <= r, ws_ref[g], 0.0).astype(BF16)

    v = v_ref[...].astype(F32)
    mu = jnp.mean(v, axis=-1, keepdims=True)
    var = jnp.mean(jnp.square(v - mu), axis=-1, keepdims=True)
    vn_ref[...] = (((v - mu) * lax.rsqrt(var + EPS)) * gv_ref[...] + bv_ref[...]).astype(BF16)
    cw = C_WIDTH // C_GROUPS
    chunks = [slice(ch * CHUNK, (ch + 1) * CHUNK) for ch in range(tm // CHUNK)]
    for g in range(C_GROUPS):
        cols = slice(g * cw, (g + 1) * cw)
        sv = _dot(wt_ref[g], jnp.concatenate([vn_ref[rows, cols] for rows in chunks], axis=1))
        for ch, rows in enumerate(chunks):
            mixed = sv[:, ch * cw:(ch + 1) * cw] + bs_ref[g]
            gated_ref[rows, cols] = (u_ref[rows, cols].astype(F32) * mixed).astype(BF16)
    y = _dot(gated_ref[...], wo_ref[...])
    o_ref[...] = h_ref[...] + _rms(y, g_ref[...])


def _gmlp_tail(z, gv, bv, ws, bs_b, wo, g, h, *, tm=ROW_BLOCK):
    t, d = h.shape
    cw = C_WIDTH // C_GROUPS
    return pl.pallas_call(
        functools.partial(_gmlp_tail_kernel, tm=tm),
        grid=(t // tm,),
        in_specs=[
            pl.BlockSpec((tm, C_WIDTH), lambda i: (i, 0)),
            pl.BlockSpec((tm, C_WIDTH), lambda i: (i, 1)),
            pl.BlockSpec((1, C_WIDTH), lambda i: (0, 0)),
            pl.BlockSpec((1, C_WIDTH), lambda i: (0, 0)),
            pl.BlockSpec((C_GROUPS, CHUNK, CHUNK), lambda i: (0, 0, 0)),
            pl.BlockSpec((C_GROUPS, CHUNK, cw), lambda i: (0, 0, 0)),
            pl.BlockSpec((C_WIDTH, d), lambda i: (0, 0)),
            pl.BlockSpec((1, d), lambda i: (0, 0)),
            pl.BlockSpec((tm, d), lambda i: (i, 0)),
        ],
        out_specs=pl.BlockSpec((tm, d), lambda i: (i, 0)),
        out_shape=jax.ShapeDtypeStruct((t, d), F32),
        scratch_shapes=[
            pltpu.VMEM((C_GROUPS, CHUNK, CHUNK), BF16),
            pltpu.VMEM((tm, C_WIDTH), BF16),
            pltpu.VMEM((tm, C_WIDTH), BF16),
        ],
        compiler_params=_params("arbitrary"),
        name="gmlp_tail",
    )(z, z, gv, bv, ws, bs_b, wo, g, h)


def _prep_even_weights(w):
    split_points = np.cumsum(EVEN_WIDTHS)[:-1].tolist()
    qa, ka, va, qi, ki, wi, qb, kb, vb = jnp.split(w, split_points, axis=1)
    d = w.shape[0]
    main = jnp.concatenate(
        [qa * (HEAD_DIM ** -0.5), qi * (IDX_DIM ** -0.5), ka, va, ki, jnp.zeros((d, LANES - IDX_DIM), w.dtype),
         qb * (HEAD_DIM ** -0.5 * LOG2_E), kb, vb], axis=1).astype(BF16)
    wi_cols = jnp.concatenate(
        [wi * (IDX_HEADS ** -0.5), jnp.zeros((d, LANES - IDX_HEADS), w.dtype)], axis=1).astype(BF16)
    return main, wi_cols


def kernel(x, norm_gains, w_in_even, w_out_even, w_in_odd, spatial_w, spatial_b, v_norm_gain, v_norm_bias,
           w_out_odd, w_ffn_in, w_ffn_out):
    b, s, d = x.shape
    t = b * s
    depth = norm_gains.shape[0]
    h = x.reshape(t, d)
    tm = ROW_BLOCK
    mix_a = N_HEADS_A * HEAD_DIM
    for layer in range(depth):
        g = norm_gains[layer][:, None, :]
        i = layer // 2
        if layer % 2 == 0:
            w_main, w_wi = _prep_even_weights(w_in_even[i])
            proj, wi = _norm_matmul(h, g[0], w_main, act="none", tm=tm, tn=EVEN_COLS // 5, side_w=w_wi)
            proj = proj.reshape(b, s, EVEN_COLS)
            ya = _dsa_attention(proj, wi.reshape(b, s, LANES))
            yb = _sb_attention(proj)
            wo = w_out_even[i].astype(BF16)
            h = _mm_norm_res([ya.reshape(t, mix_a), yb.reshape(t, -1)], [wo[:mix_a], wo[mix_a:]], g[1], h, tm=tm)
        else:
            z = _norm_matmul(h, g[0], w_in_odd[i].astype(BF16), act="gelu", tm=tm, tn=1024)
            bs_b = jnp.broadcast_to(spatial_b[i][:, :, None], (C_GROUPS, CHUNK, C_WIDTH // C_GROUPS))
            h = _gmlp_tail(z, v_norm_gain[i][None, :], v_norm_bias[i][None, :], spatial_w[i], bs_b,
                           w_out_odd[i].astype(BF16), g[1], h)
        h = _ffn(h, g[2], w_ffn_in[layer].astype(BF16), w_ffn_out[layer].astype(BF16), g[3],
                 tm=ROW_BLOCK, tn=D_FF // 11)
    return h.reshape(b, s, d)
```
